```python
import math
import jax, jax.numpy as jnp
from jax import lax
import numpy as np

D_MODEL = 1024
BATCH = 8
SEQ = 2048
DEPTH = 4

N_MLA_HEADS = 8
QK_NOPE_DIM = 64
QK_ROPE_DIM = 32
QK_HEAD_DIM = QK_NOPE_DIM + QK_ROPE_DIM
V_HEAD_DIM = 64
Q_LORA_RANK = 256
KV_LORA_RANK = 256
ROPE_THETA = 10000.0
Q_BLOCK = 128
LRU_WIDTH = D_MODEL
N_LRU_BLOCKS = 8
LRU_BLOCK = LRU_WIDTH // N_LRU_BLOCKS
CONV_WIDTH = 4
CONV_LEFT = 2
RG_LRU_C = 8.0
A_MIN_RAD = 0.9
A_MAX_RAD = 0.999
D_FF = 2816
N_EXPERTS = 8
TOP_K = 2
N_DENSE = (DEPTH + 1) // 2
N_MOE = DEPTH // 2
EPS = 1e-6

IN_SIZES = (Q_LORA_RANK, KV_LORA_RANK, QK_ROPE_DIM, LRU_WIDTH, LRU_WIDTH, 2 * D_MODEL)
IN_COLS = sum(IN_SIZES)
IN_SPLITS = [int(v) for v in np.cumsum(IN_SIZES)[:-1]]

kernel_name = "hybrid_rglru_mla_moe_adaln_encoder"


def rms_norm(x, g):
    x32 = x.astype(jnp.float32)
    y = x32 * lax.rsqrt(jnp.mean(x32 * x32, axis=-1, keepdims=True) + EPS)
    return (y * g.astype(jnp.float32)).astype(x.dtype)


def rope_tables(positions, dtype):
    inv_freq = 1.0 / (ROPE_THETA ** (jnp.arange(0, QK_ROPE_DIM, 2, dtype=jnp.float32) / QK_ROPE_DIM))
    ang = positions.astype(jnp.float32)[..., None] * inv_freq
    return jnp.cos(ang).astype(dtype), jnp.sin(ang).astype(dtype)


def rope(x, cos, sin):
    x1, x2 = jnp.split(x, 2, axis=-1)
    return jnp.concatenate([x1 * cos - x2 * sin, x1 * sin + x2 * cos], axis=-1)


def block_attention(q, k, v):
    B, S, H, Dh = q.shape
    scale = Dh ** -0.5
    kh = k.transpose(0, 2, 1, 3)
    vh = v.transpose(0, 2, 1, 3)
    qb = q.reshape(B, S // Q_BLOCK, Q_BLOCK, H, Dh).transpose(1, 0, 3, 2, 4)

    def one_block(q_blk):
        s = jnp.einsum('bhqd,bhkd->bhqk', q_blk, kh).astype(jnp.float32) * scale
        p = jax.nn.softmax(s, axis=-1)
        return jnp.einsum('bhqk,bhkd->bhqd', p.astype(vh.dtype), vh)

    o = lax.map(one_block, qb)
    return o.transpose(1, 0, 3, 2, 4).reshape(B, S, H, -1)


def mla_branch(cq, ckv, kr, cos, sin, q_norm_g, kv_norm_g, w_uq, w_ukv, q_head_g, k_head_g, w_o):
    B, S, _ = cq.shape
    q = (rms_norm(cq, q_norm_g) @ w_uq).reshape(B, S, N_MLA_HEADS, QK_HEAD_DIM)
    kv = (rms_norm(ckv, kv_norm_g) @ w_ukv).reshape(B, S, N_MLA_HEADS, QK_NOPE_DIM + V_HEAD_DIM)
    q_nope, q_pe = q[..., :QK_NOPE_DIM], q[..., QK_NOPE_DIM:]
    k_nope, v = kv[..., :QK_NOPE_DIM], kv[..., QK_NOPE_DIM:]
    q_pe = rope(q_pe, cos[:, :, None, :], sin[:, :, None, :])
    k_pe = rope(kr, cos, sin)
    q = jnp.concatenate([q_nope, q_pe], axis=-1)
    k = jnp.concatenate(
        [k_nope, jnp.broadcast_to(k_pe[:, :, None, :], (B, S, N_MLA_HEADS, QK_ROPE_DIM))], axis=-1)
    q = rms_norm(q, q_head_g)
    k = rms_norm(k, k_head_g)
    o = block_attention(q, k, v)
    return o.reshape(B, S, N_MLA_HEADS * V_HEAD_DIM) @ w_o


def lru_combine(left, right):
    a_l, b_l = left
    a_r, b_r = right
    return a_l * a_r, a_r * b_l + b_r


def rg_lru(u, w_gates, b_gates, a_param, reverse):
    B, S, W = u.shape
    g = jnp.einsum('bsnd,gnde->gbsne', u.reshape(B, S, N_LRU_BLOCKS, LRU_BLOCK), w_gates)
    g = (g.reshape(2, B, S, W) + b_gates[:, None, None, :]).astype(jnp.float32)
    r = jax.nn.sigmoid(g[0])
    i = jax.nn.sigmoid(g[1])
    log_a = -RG_LRU_C * r * jax.nn.softplus(a_param.astype(jnp.float32))
    a = jnp.exp(log_a)
    mult = jnp.sqrt(-jnp.expm1(2.0 * log_a))
    first = jnp.arange(S) == (S - 1 if reverse else 0)
    mult = jnp.where(first[None, :, None], 1.0, mult)
    b = mult * i * u.astype(jnp.float32)
    _, h = lax.associative_scan(lru_combine, (a, b), axis=1, reverse=reverse)
    return h.astype(u.dtype)


def lru_branch(u, y_gate, conv_w, conv_b, w_gates, b_gates, a_param, w_o):
    S = u.shape[1]
    up = jnp.pad(u, ((0, 0), (CONV_LEFT, CONV_WIDTH - 1 - CONV_LEFT), (0, 0)))
    uc = conv_b
    for tap in range(CONV_WIDTH):
        uc = uc + up[:, tap:tap + S] * conv_w[tap]
    h = rg_lru(uc, w_gates[0], b_gates[0], a_param[0], False) + \
        rg_lru(uc, w_gates[1], b_gates[1], a_param[1], True)
    return (h * jax.nn.gelu(y_gate)) @ w_o


def swiglu(h, w_gate, w_up, w_down):
    return (jax.nn.silu(h @ w_gate) * (h @ w_up)) @ w_down


def moe_ffn(h, w_router, w_gate, w_up, w_down):
    B, S, D = h.shape
    t = h.reshape(B * S, D)
    logits = (t @ w_router).astype(jnp.float32)
    top_v, top_i = lax.top_k(logits, TOP_K)
    top_w = jax.nn.softmax(top_v, axis=-1)
    combine = jnp.sum(jax.nn.one_hot(top_i, N_EXPERTS, dtype=jnp.float32) * top_w[..., None], axis=1)
    combine = combine.astype(t.dtype)
    out = jnp.zeros_like(t)
    for e in range(N_EXPERTS):
        out = out + combine[:, e:e + 1] * swiglu(t, w_gate[e], w_up[e], w_down[e])
    return out.reshape(B, S, D)


def setup_inputs(seed: int = 0) -> dict:
    key = jax.random.key(seed)
    ks = jax.random.split(key, 32)

    def nrm(k, shape, scale):
        return jax.random.normal(k, shape, jnp.float32) * scale

    def gain(k, shape):
        return 1.0 + 0.02 * jax.random.normal(k, shape, jnp.float32)

    L = DEPTH
    unif = jax.random.uniform(ks[20], (L, 2, LRU_WIDTH), jnp.float32, A_MIN_RAD ** 2, A_MAX_RAD ** 2)
    lru_a_param = jnp.log(jnp.expm1(-0.5 * jnp.log(unif)))
    return {
        "x": nrm(ks[0], (BATCH, SEQ, D_MODEL), 1.0),
        "c": nrm(ks[1], (BATCH, D_MODEL), 1.0),
        "positions": jnp.broadcast_to(jnp.arange(SEQ, dtype=jnp.int32), (BATCH, SEQ)),
        "ada_w": nrm(ks[2], (L, D_MODEL, 6 * D_MODEL), 0.5 * D_MODEL ** -0.5),
        "ada_b": nrm(ks[3], (L, 6 * D_MODEL), 0.02),
        "norm1_g": gain(ks[4], (L, D_MODEL)),
        "norm2_g": gain(ks[5], (L, D_MODEL)),
        "w_in": nrm(ks[6], (L, D_MODEL, IN_COLS), D_MODEL ** -0.5),
        "q_norm_g": gain(ks[7], (L, Q_LORA_RANK)),
        "kv_norm_g": gain(ks[8], (L, KV_LORA_RANK)),
        "w_uq": nrm(ks[9], (L, Q_LORA_RANK, N_MLA_HEADS * QK_HEAD_DIM), Q_LORA_RANK ** -0.5),
        "w_ukv": nrm(ks[10], (L, KV_LORA_RANK, N_MLA_HEADS * (QK_NOPE_DIM + V_HEAD_DIM)), KV_LORA_RANK ** -0.5),
        "q_head_g": gain(ks[11], (L, QK_HEAD_DIM)),
        "k_head_g": gain(ks[12], (L, QK_HEAD_DIM)),
        "w_o_mla": nrm(ks[13], (L, N_MLA_HEADS * V_HEAD_DIM, D_MODEL), (N_MLA_HEADS * V_HEAD_DIM) ** -0.5),
        "conv_w": nrm(ks[14], (L, CONV_WIDTH, LRU_WIDTH), CONV_WIDTH ** -0.5),
        "conv_b": nrm(ks[15], (L, LRU_WIDTH), 0.01),
        "lru_gate_w": nrm(ks[16], (L, 2, 2, N_LRU_BLOCKS, LRU_BLOCK, LRU_BLOCK), LRU_BLOCK ** -0.5),
        "lru_gate_b": nrm(ks[17], (L, 2, 2, LRU_WIDTH), 0.01),
        "lru_a_param": lru_a_param,
        "w_o_lru": nrm(ks[18], (L, LRU_WIDTH, D_MODEL), LRU_WIDTH ** -0.5),
        "w_out": nrm(ks[19], (L, D_MODEL, D_MODEL), D_MODEL ** -0.5),
        "ffn_w_gate": nrm(ks[21], (N_DENSE, D_MODEL, D_FF), D_MODEL ** -0.5),
        "ffn_w_up": nrm(ks[22], (N_DENSE, D_MODEL, D_FF), D_MODEL ** -0.5),
        "ffn_w_down": nrm(ks[23], (N_DENSE, D_FF, D_MODEL), D_FF ** -0.5),
        "moe_router": nrm(ks[24], (N_MOE, D_MODEL, N_EXPERTS), D_MODEL ** -0.5),
        "moe_w_gate": nrm(ks[25], (N_MOE, N_EXPERTS, D_MODEL, D_FF), D_MODEL ** -0.5),
        "moe_w_up": nrm(ks[26], (N_MOE, N_EXPERTS, D_MODEL, D_FF), D_MODEL ** -0.5),
        "moe_w_down": nrm(ks[27], (N_MOE, N_EXPERTS, D_FF, D_MODEL), D_FF ** -0.5),
    }


def reference(x, c, positions, ada_w, ada_b, norm1_g, norm2_g, w_in, q_norm_g, kv_norm_g,
              w_uq, w_ukv, q_head_g, k_head_g, w_o_mla, conv_w, conv_b, lru_gate_w, lru_gate_b,
              lru_a_param, w_o_lru, w_out, ffn_w_gate, ffn_w_up, ffn_w_down,
              moe_router, moe_w_gate, moe_w_up, moe_w_down):
    cos, sin = rope_tables(positions, x.dtype)
    c_act = jax.nn.silu(c)
    for l in range(DEPTH):
        mod = c_act @ ada_w[l] + ada_b[l]
        shift1, scale1, gate1, shift2, scale2, gate2 = [m[:, None, :] for m in jnp.split(mod, 6, axis=-1)]

        h = rms_norm(x, norm1_g[l]) * (1 + scale1) + shift1
        cq, ckv, kr, u, y_gate, g_branch = jnp.split(h @ w_in[l], IN_SPLITS, axis=-1)
        y_mla = mla_branch(cq, ckv, kr, cos, sin, q_norm_g[l], kv_norm_g[l], w_uq[l], w_ukv[l],
                           q_head_g[l], k_head_g[l], w_o_mla[l])
        y_lru = lru_branch(u, y_gate, conv_w[l], conv_b[l], lru_gate_w[l], lru_gate_b[l],
                           lru_a_param[l], w_o_lru[l])
        g_lru, g_mla = jnp.split(jax.nn.sigmoid(g_branch), 2, axis=-1)
        x = x + gate1 * ((g_lru * y_lru + g_mla * y_mla) @ w_out[l])

        h = rms_norm(x, norm2_g[l]) * (1 + scale2) + shift2
        if l % 2 == 0:
            f = swiglu(h, ffn_w_gate[l // 2], ffn_w_up[l // 2], ffn_w_down[l // 2])
        else:
            f = moe_ffn(h, moe_router[l // 2], moe_w_gate[l // 2], moe_w_up[l // 2], moe_w_down[l // 2])
        x = x + gate2 * f
    return x
```

```python
import functools

import jax
import jax.numpy as jnp
from jax import lax
from jax.experimental import pallas as pl
from jax.experimental.pallas import tpu as pltpu

F32 = jnp.float32
BF16 = jnp.bfloat16

D_MODEL = 1024
BATCH = 8
SEQ = 2048
TOKENS = BATCH * SEQ
DEPTH = 4
N_HEADS = 8
NOPE = 64
ROPE = 32
QK_DIM = NOPE + ROPE
V_DIM = 64
Q_RANK = 256
KV_RANK = 256
LRU_W = D_MODEL
N_LRU_BLOCKS = 8
LRU_BLOCK = LRU_W // N_LRU_BLOCKS
RG_LRU_C = 8.0
D_FF = 2816
N_EXPERTS = 8
EPS = 1e-6

LANES = 128
SUBLANES = 8
HEAD_TILE = LANES

MLA_COLS = Q_RANK + KV_RANK + 2 * HEAD_TILE
IN_COLS = MLA_COLS + 2 * LRU_W + 2 * D_MODEL

TM_PROJ = 512
TQ = 256
TM_FFN = 1024
TF = 256
TM_MOE = 512
N_MOE_TILES = (2 * TOKENS) // TM_MOE + N_EXPERTS
MOE_ROWS = N_MOE_TILES * TM_MOE
TM_ROUTE = 512
TM_DISPATCH = 512
TM_COMBINE = 256

MIB = 1024 * 1024


def _params(sem, vmem_mib):
    return pltpu.CompilerParams(dimension_semantics=sem, vmem_limit_bytes=vmem_mib * MIB)


def _dot(a, b):
    return jnp.dot(a, b, preferred_element_type=F32)


def _split_bf16(a):
    hi = a.astype(BF16)
    lo = (a - hi.astype(F32)).astype(BF16)
    return hi, lo


def _rms(v, inv_n):
    return v * lax.rsqrt(jnp.sum(v * v, axis=-1, keepdims=True) * inv_n + EPS)


TN_MOD = 1536


def _mod_kernel(c_ref, w_ref, b_ref, o_ref):
    c = c_ref[...]
    ch, cl = _split_bf16(c * jax.nn.sigmoid(c))
    wh, wl = _split_bf16(w_ref[0])
    o_ref[0] = _dot(ch, wh) + (_dot(ch, wl) + _dot(cl, wh)) + b_ref[0]


def _modulation(c, ada_w, ada_b):
    n = 6 * D_MODEL
    return pl.pallas_call(
        _mod_kernel,
        grid=(DEPTH, n // TN_MOD),
        in_specs=[
            pl.BlockSpec((BATCH, D_MODEL), lambda l, j: (0, 0)),
            pl.BlockSpec((1, D_MODEL, TN_MOD), lambda l, j: (l, 0, j)),
            pl.BlockSpec((1, 1, TN_MOD), lambda l, j: (l, 0, j)),
        ],
        out_specs=pl.BlockSpec((1, BATCH, TN_MOD), lambda l, j: (l, 0, j)),
        out_shape=jax.ShapeDtypeStruct((DEPTH, BATCH, n), F32),
        compiler_params=_params(("arbitrary", "arbitrary"), 40),
        name="adaln_mod",
    )(c, ada_w, ada_b.reshape(DEPTH, 1, n))


def _inproj_kernel(x_ref, sh_ref, sc_ref, g_ref, w_ref, mla_ref, u_ref, y_ref, gb_ref):
    x = x_ref[...]
    h = _rms(x, 1.0 / D_MODEL) * g_ref[...]
    h = (h * (1.0 + sc_ref[0]) + sh_ref[0]).astype(BF16)
    c0, c1, c2 = MLA_COLS, MLA_COLS + LRU_W, MLA_COLS + 2 * LRU_W
    mla_ref[...] = _dot(h, w_ref[:, 0:c0])
    u_ref[...] = _dot(h, w_ref[:, c0:c1])
    y_ref[...] = _dot(h, w_ref[:, c1:c2]).astype(BF16)
    gb_ref[...] = _dot(h, w_ref[:, c2:IN_COLS]).astype(BF16)


def _batch_vec_spec(tm):
    per_batch = SEQ // tm
    return pl.BlockSpec((1, 1, D_MODEL), lambda i: (i // per_batch, 0, 0))


def _in_projection(x, shift, scale, gain, w):
    tm = TM_PROJ
    row = lambda n: pl.BlockSpec((tm, n), lambda i: (i, 0))
    full = lambda a: pl.BlockSpec(a.shape, lambda i: (0,) * a.ndim)
    return pl.pallas_call(
        _inproj_kernel,
        grid=(TOKENS // tm,),
        in_specs=[row(D_MODEL), _batch_vec_spec(tm), _batch_vec_spec(tm), full(gain), full(w)],
        out_specs=[row(MLA_COLS), row(LRU_W), row(LRU_W), row(2 * D_MODEL)],
        out_shape=[
            jax.ShapeDtypeStruct((TOKENS, MLA_COLS), F32),
            jax.ShapeDtypeStruct((TOKENS, LRU_W), F32),
            jax.ShapeDtypeStruct((TOKENS, LRU_W), BF16),
            jax.ShapeDtypeStruct((TOKENS, 2 * D_MODEL), BF16),
        ],
        compiler_params=_params(("arbitrary",), 56),
        name="in_proj",
    )(x, shift, scale, gain, w)


def _mla_prep_kernel(m_ref, cos_ref, sin_ref, gq_ref, gkv_ref, wq_ref, wkv_ref, hq_ref, hk_ref,
                     q_ref, k_ref, v_ref):
    cos = cos_ref[...]
    sin = sin_ref[...]
    cq = m_ref[:, 0:Q_RANK]
    ckv = m_ref[:, Q_RANK:Q_RANK + KV_RANK]
    kr_a = m_ref[:, Q_RANK + KV_RANK:Q_RANK + KV_RANK + HEAD_TILE]
    kr_b = m_ref[:, Q_RANK + KV_RANK + HEAD_TILE:MLA_COLS]
    cqn = (_rms(cq, 1.0 / Q_RANK) * gq_ref[...]).astype(BF16)
    ckvn = (_rms(ckv, 1.0 / KV_RANK) * gkv_ref[...]).astype(BF16)
    qq = _dot(cqn, wq_ref[...])
    kv = _dot(ckvn, wkv_ref[...])
    k_pe = kr_a * cos + kr_b * sin
    nq = N_HEADS * HEAD_TILE
    for h in range(N_HEADS):
        lo, hi = h * HEAD_TILE, (h + 1) * HEAD_TILE
        qh = qq[:, lo:hi] * cos + qq[:, nq + lo:nq + hi] * sin
        q_ref[0, h] = (_rms(qh, 1.0 / QK_DIM) * hq_ref[...]).astype(BF16)
        kh = kv[:, lo:hi] + k_pe
        k_ref[0, h] = (_rms(kh, 1.0 / QK_DIM) * hk_ref[...]).astype(BF16)
        v_ref[0, h] = kv[:, nq + lo:nq + hi].astype(BF16)


def _mla_prep(mla_in, cos_t, sin_t, gq, gkv, wq, wkv, hq, hk):
    tm = TM_PROJ
    per_batch = SEQ // tm
    row = lambda n: pl.BlockSpec((tm, n), lambda i: (i, 0))
    full = lambda a: pl.BlockSpec(a.shape, lambda i: (0,) * a.ndim)
    head = pl.BlockSpec((1, N_HEADS, tm, HEAD_TILE), lambda i: (i // per_batch, 0, i % per_batch, 0))
    shp = jax.ShapeDtypeStruct((BATCH, N_HEADS, SEQ, HEAD_TILE), BF16)
    return pl.pallas_call(
        _mla_prep_kernel,
        grid=(TOKENS // tm,),
        in_specs=[row(MLA_COLS), row(HEAD_TILE), row(HEAD_TILE), full(gq), full(gkv), full(wq), full(wkv),
                  full(hq), full(hk)],
        out_specs=[head, head, head],
        out_shape=[shp, shp, shp],
        compiler_params=_params(("arbitrary",), 48),
        name="mla_prep",
    )(mla_in, cos_t, sin_t, gq, gkv, wq, wkv, hq, hk)


def _attn_kernel(q_ref, k_ref, v_ref, o_ref):
    for pair in range(N_HEADS // 2):
        acc = None
        for j in range(2):
            h = 2 * pair + j
            s = lax.dot_general(q_ref[0, h], k_ref[0, h], (((1,), (1,)), ((), ())),
                                preferred_element_type=F32)
            p = jnp.exp(s - jnp.max(s, axis=-1, keepdims=True))
            inv = 1.0 / jnp.sum(p, axis=-1, keepdims=True)
            o = _dot(p.astype(BF16), v_ref[0, h]) * inv
            acc = o if acc is None else acc + o
        o_ref[0, :, pair * LANES:(pair + 1) * LANES] = acc.astype(BF16)


def _attention(q, k, v):
    nq = SEQ // TQ
    kv_spec = pl.BlockSpec((1, N_HEADS, SEQ, HEAD_TILE), lambda b, i: (b, 0, 0, 0))
    return pl.pallas_call(
        _attn_kernel,
        grid=(BATCH, nq),
        in_specs=[pl.BlockSpec((1, N_HEADS, TQ, HEAD_TILE), lambda b, i: (b, 0, i, 0)), kv_spec, kv_spec],
        out_specs=pl.BlockSpec((1, TQ, N_HEADS * V_DIM), lambda b, i: (b, i, 0)),
        out_shape=jax.ShapeDtypeStruct((BATCH, SEQ, N_HEADS * V_DIM), BF16),
        compiler_params=_params(("arbitrary", "arbitrary"), 48),
        name="attention",
    )(q, k, v)


N_TIME_TILES = SEQ // SUBLANES


def _gelu_tanh(x):
    return 0.5 * x * (1.0 + jnp.tanh(0.7978845608028654 * (x + 0.044715 * x * x * x)))


def _lru_kernel(u_ref, y_ref, cw_ref, cb_ref, wg_ref, bg_ref, ap_ref, o_ref,
                pad_ref, uc_ref, g_ref, hf_ref, hb_ref):
    zeros = jnp.zeros((SUBLANES, LRU_BLOCK), F32)
    pad_ref[0:SUBLANES, :] = zeros
    pad_ref[SEQ + SUBLANES:SEQ + 2 * SUBLANES, :] = zeros
    pad_ref[SUBLANES:SEQ + SUBLANES, :] = u_ref[0]
    uc = cb_ref[...]
    for tap in range(4):
        uc = uc + pad_ref[SUBLANES - 2 + tap:SUBLANES - 2 + tap + SEQ, :] * cw_ref[tap:tap + 1, :]
    uc_ref[...] = uc
    g_ref[...] = _dot(uc.astype(BF16), wg_ref[0]) + bg_ref[0]

    sp = jax.nn.softplus(ap_ref[0])
    coef_f = -RG_LRU_C * sp[:, 0:LRU_BLOCK]
    coef_b = -RG_LRU_C * sp[:, LRU_BLOCK:2 * LRU_BLOCK]
    row = lax.broadcasted_iota(jnp.int32, (SUBLANES, LRU_BLOCK), 0)

    def gate(g_r, g_i, ucv, coef, first):
        log_a = coef * jax.nn.sigmoid(g_r)
        a = jnp.exp(log_a)
        t = jnp.tanh(log_a)
        mult = jnp.where(first, 1.0, jnp.sqrt(-2.0 * t / (1.0 - t)))
        return a, mult * jax.nn.sigmoid(g_i) * ucv

    def scan_tile(a, b, reverse):
        for s in (1, 2, 4):
            shift = SUBLANES - s if reverse else s
            a_s = pltpu.roll(a, shift, 0)
            b_s = pltpu.roll(b, shift, 0)
            keep = (row < SUBLANES - s) if reverse else (row >= s)
            b = jnp.where(keep, a * b_s + b, b)
            a = jnp.where(keep, a * a_s, a)
        return a, b

    def body(k, carry):
        hf_prev, hb_prev = carry
        of = pl.multiple_of(k * SUBLANES, SUBLANES)
        ob = pl.multiple_of((N_TIME_TILES - 1 - k) * SUBLANES, SUBLANES)
        first = k == 0
        a, b = gate(g_ref[pl.ds(of, SUBLANES), 0:LRU_BLOCK], g_ref[pl.ds(of, SUBLANES), LRU_BLOCK:2 * LRU_BLOCK],
                    uc_ref[pl.ds(of, SUBLANES), :], coef_f, jnp.logical_and(first, row == 0))
        a, b = scan_tile(a, b, False)
        hf = a * hf_prev + b
        hf_ref[pl.ds(of, SUBLANES), :] = hf
        a, b = gate(g_ref[pl.ds(ob, SUBLANES), 2 * LRU_BLOCK:3 * LRU_BLOCK],
                    g_ref[pl.ds(ob, SUBLANES), 3 * LRU_BLOCK:4 * LRU_BLOCK],
                    uc_ref[pl.ds(ob, SUBLANES), :], coef_b, jnp.logical_and(first, row == SUBLANES - 1))
        a, b = scan_tile(a, b, True)
        hb = a * hb_prev + b
        hb_ref[pl.ds(ob, SUBLANES), :] = hb
        return (jnp.broadcast_to(hf[SUBLANES - 1:SUBLANES, :], (SUBLANES, LRU_BLOCK)),
                jnp.broadcast_to(hb[0:1, :], (SUBLANES, LRU_BLOCK)))

    lax.fori_loop(0, N_TIME_TILES, body, (zeros, zeros), unroll=2)
    o_ref[0] = ((hf_ref[...] + hb_ref[...]) * _gelu_tanh(y_ref[0].astype(F32))).astype(BF16)


def _lru_branch(u, y_gate, conv_w, conv_b, wg, bg, a_param):
    blk = lambda: pl.BlockSpec((1, SEQ, LRU_BLOCK), lambda b, n: (b, 0, n))
    scratch = lambda rows, cols: pltpu.VMEM((rows, cols), F32)
    return pl.pallas_call(
        _lru_kernel,
        grid=(BATCH, N_LRU_BLOCKS),
        in_specs=[
            blk(), blk(),
            pl.BlockSpec((4, LRU_BLOCK), lambda b, n: (0, n)),
            pl.BlockSpec((1, LRU_BLOCK), lambda b, n: (0, n)),
            pl.BlockSpec((1, LRU_BLOCK, 4 * LRU_BLOCK), lambda b, n: (n, 0, 0)),
            pl.BlockSpec((1, 1, 4 * LRU_BLOCK), lambda b, n: (n, 0, 0)),
            pl.BlockSpec((1, 1, 2 * LRU_BLOCK), lambda b, n: (n, 0, 0)),
        ],
        out_specs=blk(),
        out_shape=jax.ShapeDtypeStruct((BATCH, SEQ, LRU_W), BF16),
        scratch_shapes=[scratch(SEQ + 2 * SUBLANES, LRU_BLOCK), scratch(SEQ, LRU_BLOCK),
                        scratch(SEQ, 4 * LRU_BLOCK), scratch(SEQ, LRU_BLOCK), scratch(SEQ, LRU_BLOCK)],
        compiler_params=_params(("arbitrary", "arbitrary"), 40),
        name="rg_lru",
    )(u, y_gate, conv_w, conv_b, wg, bg, a_param)


def _mix_kernel(*refs, moe):
    if moe:
        (hg_ref, at_ref, gb_ref, x_ref, g1_ref, sh_ref, sc_ref, n2_ref, wol_ref, wom_ref, wout_ref,
         wrh_ref, wrl_ref, xo_ref, h2_ref, lg_ref) = refs
    else:
        (hg_ref, at_ref, gb_ref, x_ref, g1_ref, sh_ref, sc_ref, n2_ref, wol_ref, wom_ref, wout_ref,
         xo_ref, h2_ref) = refs
    y_lru = _dot(hg_ref[...], wol_ref[...])
    y_mla = _dot(at_ref[...], wom_ref[...])
    g_lru = jax.nn.sigmoid(gb_ref[:, 0:D_MODEL].astype(F32))
    g_mla = jax.nn.sigmoid(gb_ref[:, D_MODEL:2 * D_MODEL].astype(F32))
    z = (g_lru * y_lru + g_mla * y_mla).astype(BF16)
    xn = x_ref[...] + g1_ref[0] * _dot(z, wout_ref[...])
    xo_ref[...] = xn
    h2 = _rms(xn, 1.0 / D_MODEL) * n2_ref[...]
    h2 = h2 * (1.0 + sc_ref[0]) + sh_ref[0]
    if moe:
        h2_ref[...] = h2
        hh, hl = _split_bf16(h2)
        lg_ref[...] = _dot(hh, wrh_ref[...]) + (_dot(hh, wrl_ref[...]) + _dot(hl, wrh_ref[...]))
    else:
        h2_ref[...] = h2.astype(BF16)


def _mix(hg, attn, gb, x, gate1, shift2, scale2, n2, wol, wom, wout, router=None):
    tm = TM_PROJ
    moe = router is not None
    row = lambda n: pl.BlockSpec((tm, n), lambda i: (i, 0))
    full = lambda a: pl.BlockSpec(a.shape, lambda i: (0,) * a.ndim)
    vec = _batch_vec_spec(tm)
    ins = [hg, attn, gb, x, gate1, shift2, scale2, n2, wol, wom, wout]
    in_specs = [row(LRU_W), row(N_HEADS * V_DIM), row(2 * D_MODEL), row(D_MODEL), vec, vec, vec,
                full(n2), full(wol), full(wom), full(wout)]
    out_specs = [row(D_MODEL), row(D_MODEL)]
    out_shape = [jax.ShapeDtypeStruct((TOKENS, D_MODEL), F32),
                 jax.ShapeDtypeStruct((TOKENS, D_MODEL), F32 if moe else BF16)]
    if moe:
        ins += list(router)
        in_specs += [full(router[0]), full(router[1])]
        out_specs.append(row(LANES))
        out_shape.append(jax.ShapeDtypeStruct((TOKENS, LANES), F32))
    return pl.pallas_call(
        functools.partial(_mix_kernel, moe=moe),
        grid=(TOKENS // tm,),
        in_specs=in_specs,
        out_specs=out_specs,
        out_shape=out_shape,
        compiler_params=_params(("arbitrary",), 48),
        name="mix_moe" if moe else "mix_dense",
    )(*ins)


def _swiglu_step(h, wg_ref, wu_ref, wd_ref, acc_ref, j):
    @pl.when(j == 0)
    def _():
        acc_ref[...] = jnp.zeros_like(acc_ref)

    a = _dot(h, wg_ref[...])
    b = _dot(h, wu_ref[...])
    t = (a * jax.nn.sigmoid(a) * b).astype(BF16)
    acc_ref[...] += _dot(t, wd_ref[...])


def _ffn_kernel(h_ref, x_ref, g2_ref, wg_ref, wu_ref, wd_ref, o_ref, acc_ref):
    j = pl.program_id(1)
    _swiglu_step(h_ref[...], wg_ref, wu_ref, wd_ref, acc_ref, j)

    @pl.when(j == pl.num_programs(1) - 1)
    def _():
        o_ref[...] = x_ref[...] + g2_ref[0] * acc_ref[...]


def _dense_ffn(h2, x, gate2, wg, wu, wd):
    tm = TM_FFN
    per_batch = SEQ // tm
    row = pl.BlockSpec((tm, D_MODEL), lambda i, j: (i, 0))
    return pl.pallas_call(
        _ffn_kernel,
        grid=(TOKENS // tm, D_FF // TF),
        in_specs=[row, row, pl.BlockSpec((1, 1, D_MODEL), lambda i, j: (i // per_batch, 0, 0)),
                  pl.BlockSpec((D_MODEL, TF), lambda i, j: (0, j)),
                  pl.BlockSpec((D_MODEL, TF), lambda i, j: (0, j)),
                  pl.BlockSpec((TF, D_MODEL), lambda i, j: (j, 0))],
        out_specs=row,
        out_shape=jax.ShapeDtypeStruct((TOKENS, D_MODEL), F32),
        scratch_shapes=[pltpu.VMEM((tm, D_MODEL), F32)],
        compiler_params=_params(("arbitrary", "arbitrary"), 48),
        name="ffn_dense",
    )(h2, x, gate2, wg, wu, wd)


def _route_kernel(lg_ref, meta_ref, wt_ref, cnt_ref, carry_ref):
    i = pl.program_id(0)

    @pl.when(i == 0)
    def _():
        carry_ref[...] = jnp.zeros_like(carry_ref)

    tm = TM_ROUTE
    lane = lax.broadcasted_iota(jnp.int32, (tm, LANES), 1)
    lane_f = lane.astype(F32)
    neg = jnp.float32(-jnp.inf)
    lg = jnp.where(lane < N_EXPERTS, lg_ref[...], neg)
    m1 = jnp.max(lg, axis=-1, keepdims=True)
    i1 = jnp.min(jnp.where(lg == m1, lane_f, float(LANES)), axis=-1, keepdims=True).astype(jnp.int32)
    lg2 = jnp.where(lane == i1, neg, lg)
    m2 = jnp.max(lg2, axis=-1, keepdims=True)
    i2 = jnp.min(jnp.where(lg2 == m2, lane_f, float(LANES)), axis=-1, keepdims=True).astype(jnp.int32)
    e = jnp.exp(m2 - m1)
    w1 = 1.0 / (1.0 + e)
    w2 = e * w1
    sel = jnp.where(lane == i1, 1.0, jnp.where(lane == i2, 1.0, 0.0))
    r_i = lax.broadcasted_iota(jnp.int32, (tm, tm), 0)
    c_i = lax.broadcasted_iota(jnp.int32, (tm, tm), 1)
    tri = jnp.where(r_i > c_i, 1.0, 0.0).astype(BF16)
    cum = _dot(tri, sel.astype(BF16)) + carry_ref[0:1, :]
    r1 = jnp.sum(jnp.where(lane == i1, cum, 0.0), axis=-1, keepdims=True).astype(jnp.int32)
    r2 = jnp.sum(jnp.where(lane == i2, cum, 0.0), axis=-1, keepdims=True).astype(jnp.int32)
    total = carry_ref[0:1, :] + jnp.sum(sel, axis=0, keepdims=True)
    carry_ref[...] = jnp.broadcast_to(total, carry_ref.shape)
    cnt_ref[...] = jnp.broadcast_to(total, cnt_ref.shape)
    meta_ref[...] = jnp.where(lane == 0, i1, jnp.where(lane == 1, i2, jnp.where(lane == 2, r1,
                              jnp.where(lane == 3, r2, 0))))
    wt_ref[...] = jnp.where(lane == 0, w1, jnp.where(lane == 1, w2, 0.0))


def _route(logits):
    tm = TM_ROUTE
    row = pl.BlockSpec((tm, LANES), lambda i: (i, 0))
    cnt = pl.BlockSpec((SUBLANES, LANES), lambda i: (0, 0))
    return pl.pallas_call(
        _route_kernel,
        grid=(TOKENS // tm,),
        in_specs=[row],
        out_specs=[row, row, cnt],
        out_shape=[jax.ShapeDtypeStruct((TOKENS, LANES), jnp.int32),
                   jax.ShapeDtypeStruct((TOKENS, LANES), F32),
                   jax.ShapeDtypeStruct((SUBLANES, LANES), F32)],
        scratch_shapes=[pltpu.VMEM((SUBLANES, LANES), F32)],
        compiler_params=_params(("arbitrary",), 32),
        name="route_top2",
    )(logits)


def _row_copy(src_ref, src_row, dst_ref, dst_row, sem):
    return pltpu.make_async_copy(src_ref.at[pl.ds(src_row, 1)], dst_ref.at[pl.ds(dst_row, 1)], sem)


def _dispatch_kernel(p0_ref, p1_ref, h_ref, init_ref, xs_ref, sem):
    del init_ref
    base = pl.program_id(0) * TM_DISPATCH

    def issue(r, c):
        _row_copy(h_ref, r, xs_ref, p0_ref[base + r], sem).start()
        _row_copy(h_ref, r, xs_ref, p1_ref[base + r], sem).start()
        return c

    def drain(r, c):
        _row_copy(h_ref, 0, xs_ref, 0, sem).wait()
        _row_copy(h_ref, 0, xs_ref, 0, sem).wait()
        return c

    lax.fori_loop(0, TM_DISPATCH, issue, 0)
    lax.fori_loop(0, TM_DISPATCH, drain, 0)


def _dispatch(pos0, pos1, h2):
    tm = TM_DISPATCH
    grid_spec = pltpu.PrefetchScalarGridSpec(
        num_scalar_prefetch=2,
        grid=(TOKENS // tm,),
        in_specs=[pl.BlockSpec((tm, D_MODEL), lambda i, p0, p1: (i, 0)),
                  pl.BlockSpec(memory_space=pl.ANY)],
        out_specs=pl.BlockSpec(memory_space=pl.ANY),
        scratch_shapes=[pltpu.SemaphoreType.DMA(())],
    )
    return pl.pallas_call(
        _dispatch_kernel,
        grid_spec=grid_spec,
        out_shape=jax.ShapeDtypeStruct((MOE_ROWS, D_MODEL), F32),
        input_output_aliases={3: 0},
        compiler_params=_params(("arbitrary",), 32),
        name="moe_dispatch",
    )(pos0, pos1, h2, jnp.zeros((MOE_ROWS, D_MODEL), F32))


def _moe_ffn_kernel(te_ref, valid_ref, x_ref, wg_ref, wu_ref, wd_ref, o_ref, acc_ref):
    i = pl.program_id(0)
    j = pl.program_id(1)
    last = j == pl.num_programs(1) - 1

    @pl.when(valid_ref[i] == 1)
    def _():
        _swiglu_step(x_ref[...].astype(BF16), wg_ref.at[0], wu_ref.at[0], wd_ref.at[0], acc_ref, j)

        @pl.when(last)
        def _():
            o_ref[...] = acc_ref[...]

    @pl.when(jnp.logical_and(valid_ref[i] == 0, last))
    def _():
        o_ref[...] = jnp.zeros_like(o_ref)


def _moe_ffn(tile_expert, tile_valid, xs, wg, wu, wd):
    tm = TM_MOE
    row = pl.BlockSpec((tm, D_MODEL), lambda i, j, te, va: (i, 0))
    grid_spec = pltpu.PrefetchScalarGridSpec(
        num_scalar_prefetch=2,
        grid=(N_MOE_TILES, D_FF // TF),
        in_specs=[row,
                  pl.BlockSpec((1, D_MODEL, TF), lambda i, j, te, va: (te[i], 0, j * va[i])),
                  pl.BlockSpec((1, D_MODEL, TF), lambda i, j, te, va: (te[i], 0, j * va[i])),
                  pl.BlockSpec((1, TF, D_MODEL), lambda i, j, te, va: (te[i], j * va[i], 0))],
        out_specs=row,
        scratch_shapes=[pltpu.VMEM((tm, D_MODEL), F32)],
    )
    return pl.pallas_call(
        _moe_ffn_kernel,
        grid_spec=grid_spec,
        out_shape=jax.ShapeDtypeStruct((MOE_ROWS, D_MODEL), F32),
        compiler_params=_params(("arbitrary", "arbitrary"), 40),
        name="ffn_moe",
    )(tile_expert, tile_valid, xs, wg, wu, wd)


def _moe_combine_kernel(p0_ref, p1_ref, x_ref, g2_ref, wt_ref, y_ref, o_ref, buf_ref, sem):
    tm = TM_COMBINE
    base = pl.program_id(0) * tm

    def issue(r, c):
        _row_copy(y_ref, p0_ref[base + r], buf_ref.at[0], r, sem).start()
        _row_copy(y_ref, p1_ref[base + r], buf_ref.at[1], r, sem).start()
        return c

    def drain(r, c):
        _row_copy(y_ref, 0, buf_ref.at[0], 0, sem).wait()
        _row_copy(y_ref, 0, buf_ref.at[1], 0, sem).wait()
        return c

    lax.fori_loop(0, tm, issue, 0)
    lax.fori_loop(0, tm, drain, 0)
    w = wt_ref[...]
    f = w[:, 0:1] * buf_ref[0] + w[:, 1:2] * buf_ref[1]
    o_ref[...] = x_ref[...] + g2_ref[0] * f


def _moe_combine(pos0, pos1, x, gate2, wts, y):
    tm = TM_COMBINE
    per_batch = SEQ // tm
    row = pl.BlockSpec((tm, D_MODEL), lambda i, p0, p1: (i, 0))
    grid_spec = pltpu.PrefetchScalarGridSpec(
        num_scalar_prefetch=2,
        grid=(TOKENS // tm,),
        in_specs=[row,
                  pl.BlockSpec((1, 1, D_MODEL), lambda i, p0, p1: (i // per_batch, 0, 0)),
                  pl.BlockSpec((tm, LANES), lambda i, p0, p1: (i, 0)),
                  pl.BlockSpec(memory_space=pl.ANY)],
        out_specs=row,
        scratch_shapes=[pltpu.VMEM((2, tm, D_MODEL), F32), pltpu.SemaphoreType.DMA(())],
    )
    return pl.pallas_call(
        _moe_combine_kernel,
        grid_spec=grid_spec,
        out_shape=jax.ShapeDtypeStruct((TOKENS, D_MODEL), F32),
        compiler_params=_params(("arbitrary",), 32),
        name="moe_combine",
    )(pos0, pos1, x, gate2, wts, y)


def _prep_w_in(w_in):
    cq_ckv = w_in[..., 0:Q_RANK + KV_RANK]
    kr = w_in[..., Q_RANK + KV_RANK:Q_RANK + KV_RANK + ROPE]
    rest = w_in[..., Q_RANK + KV_RANK + ROPE:]
    x1, x2 = kr[..., :ROPE // 2], kr[..., ROPE // 2:]
    z = lambda n: jnp.zeros(kr.shape[:-1] + (n,), w_in.dtype)
    kr_a = jnp.concatenate([z(NOPE), x1, x2, z(HEAD_TILE - QK_DIM)], axis=-1)
    kr_b = jnp.concatenate([z(NOPE), -x2, x1, z(HEAD_TILE - QK_DIM)], axis=-1)
    return jnp.concatenate([cq_ckv, kr_a, kr_b, rest], axis=-1).astype(BF16)


def _prep_w_uq(w_uq):
    w = w_uq.reshape(DEPTH, Q_RANK, N_HEADS, QK_DIM)
    nope, x1, x2 = w[..., :NOPE], w[..., NOPE:NOPE + ROPE // 2], w[..., NOPE + ROPE // 2:]
    z = lambda n: jnp.zeros(w.shape[:-1] + (n,), w.dtype)
    qa = jnp.concatenate([nope, x1, x2, z(HEAD_TILE - QK_DIM)], axis=-1)
    qb = jnp.concatenate([z(NOPE), -x2, x1, z(HEAD_TILE - QK_DIM)], axis=-1)
    flat = lambda a: a.reshape(DEPTH, Q_RANK, N_HEADS * HEAD_TILE)
    return jnp.concatenate([flat(qa), flat(qb)], axis=-1).astype(BF16)


def _prep_w_ukv(w_ukv):
    w = w_ukv.reshape(DEPTH, KV_RANK, N_HEADS, NOPE + V_DIM)
    k_nope, v = w[..., :NOPE], w[..., NOPE:]
    zk = jnp.zeros(k_nope.shape[:-1] + (HEAD_TILE - NOPE,), w.dtype)
    ka = jnp.concatenate([k_nope, zk], axis=-1)
    zv = jnp.zeros_like(v)
    even = (jnp.arange(N_HEADS) % 2 == 0)[None, None, :, None]
    vp = jnp.concatenate([jnp.where(even, v, zv), jnp.where(even, zv, v)], axis=-1)
    flat = lambda a: a.reshape(DEPTH, KV_RANK, N_HEADS * HEAD_TILE)
    return jnp.concatenate([flat(ka), flat(vp)], axis=-1).astype(BF16)


def _head_gain(g, scale):
    pad = jnp.zeros((DEPTH, HEAD_TILE - QK_DIM), g.dtype)
    return (jnp.concatenate([g, pad], axis=-1) * scale).reshape(DEPTH, 1, HEAD_TILE)


def _rope_tables(positions):
    inv_freq = 1.0 / (10000.0 ** (jnp.arange(0, ROPE, 2, dtype=F32) / ROPE))
    ang = positions.astype(F32).reshape(TOKENS, 1) * inv_freq
    cos, sin = jnp.cos(ang), jnp.sin(ang)
    cos_t = jnp.concatenate([jnp.ones((TOKENS, NOPE), F32), cos, cos,
                             jnp.zeros((TOKENS, HEAD_TILE - QK_DIM), F32)], axis=-1)
    sin_t = jnp.concatenate([jnp.zeros((TOKENS, NOPE), F32), sin, sin,
                             jnp.zeros((TOKENS, HEAD_TILE - QK_DIM), F32)], axis=-1)
    return cos_t, sin_t


def _moe_plan(meta, cnt):
    e1, e2, r1, r2 = meta[:, 0], meta[:, 1], meta[:, 2], meta[:, 3]
    counts = cnt[0, :N_EXPERTS].astype(jnp.int32)
    padded = ((counts + TM_MOE - 1) // TM_MOE) * TM_MOE
    ends = jnp.cumsum(padded)
    starts = ends - padded
    experts = jnp.arange(N_EXPERTS, dtype=jnp.int32)
    start_of = lambda e: jnp.sum(jnp.where(e[:, None] == experts[None, :], starts[None, :], 0), axis=1)
    pos0 = start_of(e1) + r1
    pos1 = start_of(e2) + r2
    tile_start = jnp.arange(N_MOE_TILES, dtype=jnp.int32) * TM_MOE
    tile_expert = jnp.minimum(jnp.sum(tile_start[:, None] >= ends[None, :], axis=1), N_EXPERTS - 1)
    tile_valid = (tile_start < ends[-1]).astype(jnp.int32)
    return pos0.astype(jnp.int32), pos1.astype(jnp.int32), tile_expert.astype(jnp.int32), tile_valid


def kernel(x, c, positions, ada_w, ada_b, norm1_g, norm2_g, w_in, q_norm_g, kv_norm_g, w_uq, w_ukv, q_head_g,
           k_head_g, w_o_mla, conv_w, conv_b, lru_gate_w, lru_gate_b, lru_a_param, w_o_lru, w_out, ffn_w_gate,
           ffn_w_up, ffn_w_down, moe_router, moe_w_gate, moe_w_up, moe_w_down):
    cos_t, sin_t = _rope_tables(positions)
    mod = _modulation(c, ada_w, ada_b).reshape(DEPTH, BATCH, 6, 1, D_MODEL)

    w_in_p = _prep_w_in(w_in)
    wq_p = _prep_w_uq(w_uq)
    wkv_p = _prep_w_ukv(w_ukv)
    hq = _head_gain(q_head_g, QK_DIM ** -0.5)
    hk = _head_gain(k_head_g, 1.0)
    wg_p = lru_gate_w.transpose(0, 3, 4, 1, 2, 5).reshape(
        DEPTH, N_LRU_BLOCKS, LRU_BLOCK, 4 * LRU_BLOCK).astype(BF16)
    bg_p = lru_gate_b.reshape(DEPTH, 2, 2, N_LRU_BLOCKS, LRU_BLOCK).transpose(0, 3, 1, 2, 4).reshape(
        DEPTH, N_LRU_BLOCKS, 1, 4 * LRU_BLOCK)
    ap_p = lru_a_param.reshape(DEPTH, 2, N_LRU_BLOCKS, LRU_BLOCK).transpose(0, 2, 1, 3).reshape(
        DEPTH, N_LRU_BLOCKS, 1, 2 * LRU_BLOCK)
    wol, wom, wout = w_o_lru.astype(BF16), w_o_mla.astype(BF16), w_out.astype(BF16)
    fg, fu, fd = ffn_w_gate.astype(BF16), ffn_w_up.astype(BF16), ffn_w_down.astype(BF16)
    mg, mu, md = moe_w_gate.astype(BF16), moe_w_up.astype(BF16), moe_w_down.astype(BF16)
    router = jnp.pad(moe_router, ((0, 0), (0, 0), (0, LANES - N_EXPERTS)))
    router_hi = router.astype(BF16)
    router_lo = (router - router_hi.astype(F32)).astype(BF16)

    xt = x.reshape(TOKENS, D_MODEL)
    for l in range(DEPTH):
        shift1, scale1, gate1, shift2, scale2, gate2 = [mod[l, :, k] for k in range(6)]
        mla_in, u, y_gate, gb = _in_projection(xt, shift1, scale1, norm1_g[l].reshape(1, D_MODEL), w_in_p[l])
        q, k, v = _mla_prep(mla_in, cos_t, sin_t, q_norm_g[l].reshape(1, Q_RANK),
                            kv_norm_g[l].reshape(1, KV_RANK), wq_p[l], wkv_p[l], hq[l], hk[l])
        attn = _attention(q, k, v).reshape(TOKENS, N_HEADS * V_DIM)
        hg = _lru_branch(u.reshape(BATCH, SEQ, LRU_W), y_gate.reshape(BATCH, SEQ, LRU_W), conv_w[l],
                         conv_b[l].reshape(1, LRU_W), wg_p[l], bg_p[l], ap_p[l]).reshape(TOKENS, LRU_W)
        n2 = norm2_g[l].reshape(1, D_MODEL)
        if l % 2 == 0:
            xt, h2 = _mix(hg, attn, gb, xt, gate1, shift2, scale2, n2, wol[l], wom[l], wout[l])
            xt = _dense_ffn(h2, xt, gate2, fg[l // 2], fu[l // 2], fd[l // 2])
        else:
            m = l // 2
            xt, h2, logits = _mix(hg, attn, gb, xt, gate1, shift2, scale2, n2, wol[l], wom[l], wout[l],
                                  router=(router_hi[m], router_lo[m]))
            meta, wts, cnt = _route(logits)
            pos0, pos1, tile_expert, tile_valid = _moe_plan(meta, cnt)
            xs = _dispatch(pos0, pos1, h2)
            y = _moe_ffn(tile_expert, tile_valid, xs, mg[m], mu[m], md[m])
            xt = _moe_combine(pos0, pos1, xt, gate2, wts, y)
    return xt.reshape(BATCH, SEQ, D_MODEL)
```

```python
import functools

import jax
import jax.numpy as jnp
from jax import lax
from jax.experimental import pallas as pl
from jax.experimental.pallas import tpu as pltpu

F32 = jnp.float32
BF16 = jnp.bfloat16

D_MODEL = 1024
BATCH = 8
SEQ = 2048
TOKENS = BATCH * SEQ
DEPTH = 4
N_HEADS = 8
NOPE = 64
ROPE = 32
QK_DIM = NOPE + ROPE
V_DIM = 64
Q_RANK = 256
KV_RANK = 256
LRU_W = D_MODEL
N_LRU_BLOCKS = 8
LRU_BLOCK = LRU_W // N_LRU_BLOCKS
RG_LRU_C = 8.0
D_FF = 2816
N_EXPERTS = 8
N_MOE_LAYERS = DEPTH // 2
EPS = 1e-6

LANES = 128
SUBLANES = 8
HEAD_TILE = LANES

MLA_COLS = Q_RANK + KV_RANK + 2 * HEAD_TILE
IN_COLS = MLA_COLS + 2 * LRU_W + 2 * D_MODEL

TM_PROJ = 512
TQ = 256
TM_FFN = 512
TF = D_FF // 2
N_MOE_TILES = (2 * TOKENS) // TM_FFN + N_EXPERTS
MOE_ROWS = N_MOE_TILES * TM_FFN
TM_ROUTE = 512
TM_DISPATCH = 512
TM_COMBINE = 256
DMA_UNROLL = 8

MIB = 1024 * 1024


def _params(sem, vmem_mib):
    return pltpu.CompilerParams(dimension_semantics=sem, vmem_limit_bytes=vmem_mib * MIB)


def _dot(a, b):
    return jnp.dot(a, b, preferred_element_type=F32)


def _split_bf16(a):
    hi = a.astype(BF16)
    lo = (a - hi.astype(F32)).astype(BF16)
    return hi, lo


def _rms(v, inv_n):
    return v * lax.rsqrt(jnp.sum(v * v, axis=-1, keepdims=True) * inv_n + EPS)


def _layer_spec(a, l):
    nd = a.ndim - 1
    return pl.BlockSpec((1,) + a.shape[1:], lambda i: (l,) + (0,) * nd)


TN_MOD = 1536


def _mod_kernel(c_ref, w_ref, b_ref, o_ref):
    c = c_ref[...]
    ch, cl = _split_bf16(c * jax.nn.sigmoid(c))
    wh, wl = _split_bf16(w_ref[0])
    o_ref[0] = _dot(ch, wh) + (_dot(ch, wl) + _dot(cl, wh)) + b_ref[0]


def _modulation(c, ada_w, ada_b):
    n = 6 * D_MODEL
    return pl.pallas_call(
        _mod_kernel,
        grid=(DEPTH, n // TN_MOD),
        in_specs=[
            pl.BlockSpec((BATCH, D_MODEL), lambda l, j: (0, 0)),
            pl.BlockSpec((1, D_MODEL, TN_MOD), lambda l, j: (l, 0, j)),
            pl.BlockSpec((1, 1, TN_MOD), lambda l, j: (l, 0, j)),
        ],
        out_specs=pl.BlockSpec((1, BATCH, TN_MOD), lambda l, j: (l, 0, j)),
        out_shape=jax.ShapeDtypeStruct((DEPTH, BATCH, n), F32),
        compiler_params=_params(("arbitrary", "arbitrary"), 40),
        name="adaln_mod",
    )(c, ada_w, ada_b.reshape(DEPTH, 1, n))


def _inproj_kernel(x_ref, sh_ref, sc_ref, g_ref, w_ref, mla_ref, u_ref, y_ref, gb_ref):
    x = x_ref[...]
    h = _rms(x, 1.0 / D_MODEL) * g_ref[0]
    h = (h * (1.0 + sc_ref[0]) + sh_ref[0]).astype(BF16)
    c0, c1, c2 = MLA_COLS, MLA_COLS + LRU_W, MLA_COLS + 2 * LRU_W
    mla_ref[...] = _dot(h, w_ref[0, :, 0:c0])
    u_ref[...] = _dot(h, w_ref[0, :, c0:c1])
    y_ref[...] = _dot(h, w_ref[0, :, c1:c2]).astype(BF16)
    gb_ref[...] = _dot(h, w_ref[0, :, c2:IN_COLS]).astype(BF16)


def _batch_vec_spec(tm):
    per_batch = SEQ // tm
    return pl.BlockSpec((1, 1, D_MODEL), lambda i: (i // per_batch, 0, 0))


def _in_projection(l, x, shift, scale, gain, w):
    tm = TM_PROJ
    row = lambda n: pl.BlockSpec((tm, n), lambda i: (i, 0))
    return pl.pallas_call(
        _inproj_kernel,
        grid=(TOKENS // tm,),
        in_specs=[row(D_MODEL), _batch_vec_spec(tm), _batch_vec_spec(tm), _layer_spec(gain, l),
                  _layer_spec(w, l)],
        out_specs=[row(MLA_COLS), row(LRU_W), row(LRU_W), row(2 * D_MODEL)],
        out_shape=[
            jax.ShapeDtypeStruct((TOKENS, MLA_COLS), F32),
            jax.ShapeDtypeStruct((TOKENS, LRU_W), F32),
            jax.ShapeDtypeStruct((TOKENS, LRU_W), BF16),
            jax.ShapeDtypeStruct((TOKENS, 2 * D_MODEL), BF16),
        ],
        compiler_params=_params(("arbitrary",), 56),
        name="in_proj",
    )(x, shift, scale, gain, w)


def _mla_prep_kernel(m_ref, cos_ref, sin_ref, gq_ref, gkv_ref, wq_ref, wkv_ref, hq_ref, hk_ref,
                     q_ref, k_ref, v_ref):
    cos = cos_ref[...]
    sin = sin_ref[...]
    cq = m_ref[:, 0:Q_RANK]
    ckv = m_ref[:, Q_RANK:Q_RANK + KV_RANK]
    kr_a = m_ref[:, Q_RANK + KV_RANK:Q_RANK + KV_RANK + HEAD_TILE]
    kr_b = m_ref[:, Q_RANK + KV_RANK + HEAD_TILE:MLA_COLS]
    cqn = (_rms(cq, 1.0 / Q_RANK) * gq_ref[0]).astype(BF16)
    ckvn = (_rms(ckv, 1.0 / KV_RANK) * gkv_ref[0]).astype(BF16)
    qq = _dot(cqn, wq_ref[0])
    kv = _dot(ckvn, wkv_ref[0])
    k_pe = kr_a * cos + kr_b * sin
    nq = N_HEADS * HEAD_TILE
    for h in range(N_HEADS):
        lo, hi = h * HEAD_TILE, (h + 1) * HEAD_TILE
        qh = qq[:, lo:hi] * cos + qq[:, nq + lo:nq + hi] * sin
        q_ref[0, h] = (_rms(qh, 1.0 / QK_DIM) * hq_ref[0]).astype(BF16)
        kh = kv[:, lo:hi] + k_pe
        k_ref[0, h] = (_rms(kh, 1.0 / QK_DIM) * hk_ref[0]).astype(BF16)
        v_ref[0, h] = kv[:, nq + lo:nq + hi].astype(BF16)


def _mla_prep(l, mla_in, cos_t, sin_t, gq, gkv, wq, wkv, hq, hk):
    tm = TM_PROJ
    per_batch = SEQ // tm
    row = lambda n: pl.BlockSpec((tm, n), lambda i: (i, 0))
    head = pl.BlockSpec((1, N_HEADS, tm, HEAD_TILE), lambda i: (i // per_batch, 0, i % per_batch, 0))
    shp = jax.ShapeDtypeStruct((BATCH, N_HEADS, SEQ, HEAD_TILE), BF16)
    params = [gq, gkv, wq, wkv, hq, hk]
    return pl.pallas_call(
        _mla_prep_kernel,
        grid=(TOKENS // tm,),
        in_specs=[row(MLA_COLS), row(HEAD_TILE), row(HEAD_TILE)] + [_layer_spec(p, l) for p in params],
        out_specs=[head, head, head],
        out_shape=[shp, shp, shp],
        compiler_params=_params(("arbitrary",), 48),
        name="mla_prep",
    )(mla_in, cos_t, sin_t, *params)


def _attn_kernel(q_ref, k_ref, v_ref, o_ref):
    for pair in range(N_HEADS // 2):
        acc = None
        for j in range(2):
            h = 2 * pair + j
            s = lax.dot_general(q_ref[0, h], k_ref[0, h], (((1,), (1,)), ((), ())),
                                preferred_element_type=F32)
            p = jnp.exp2(s - jnp.max(s, axis=-1, keepdims=True))
            inv = 1.0 / jnp.sum(p, axis=-1, keepdims=True)
            o = _dot(p.astype(BF16), v_ref[0, h]) * inv
            acc = o if acc is None else acc + o
        o_ref[0, :, pair * LANES:(pair + 1) * LANES] = acc.astype(BF16)


def _attention(q, k, v):
    nq = SEQ // TQ
    kv_spec = pl.BlockSpec((1, N_HEADS, SEQ, HEAD_TILE), lambda b, i: (b, 0, 0, 0))
    return pl.pallas_call(
        _attn_kernel,
        grid=(BATCH, nq),
        in_specs=[pl.BlockSpec((1, N_HEADS, TQ, HEAD_TILE), lambda b, i: (b, 0, i, 0)), kv_spec, kv_spec],
        out_specs=pl.BlockSpec((1, TQ, N_HEADS * V_DIM), lambda b, i: (b, i, 0)),
        out_shape=jax.ShapeDtypeStruct((BATCH, SEQ, N_HEADS * V_DIM), BF16),
        compiler_params=_params(("arbitrary", "arbitrary"), 48),
        name="attention",
    )(q, k, v)


CHUNK = SEQ // SUBLANES
PITCH = CHUNK + 4
CONV_BEFORE, CONV_AFTER = 2, 1
TWO_LOG2E = 2.0 * 1.4426950408889634


def _shift_rows(v, row, down):
    if down:
        return jnp.where(row == 0, 0.0, pltpu.roll(v, 1, 0))
    return jnp.where(row == SUBLANES - 1, 0.0, pltpu.roll(v, SUBLANES - 1, 0))


def _scan_sublanes(a, b, row, reverse):
    for s in (1, 2, 4):
        shift = SUBLANES - s if reverse else s
        a_s = pltpu.roll(a, shift, 0)
        b_s = pltpu.roll(b, shift, 0)
        keep = (row < SUBLANES - s) if reverse else (row >= s)
        b = jnp.where(keep, a * b_s + b, b)
        a = jnp.where(keep, a * a_s, a)
    return b


def _lru_kernel(u_hbm, cw_ref, cb_ref, wg_ref, bg_ref, ap_ref, h_hbm,
                uin_ref, hout_ref, up_ref, uc_ref, g_ref, a_ref, b_ref, h_ref, p_ref, in_sem, out_sem):
    n_steps = pl.num_programs(0) * N_LRU_BLOCKS
    step = pl.program_id(0) * N_LRU_BLOCKS + pl.program_id(1)
    slot = step % 2

    def chunk_copies(s, sl, inbound):
        bb = s // N_LRU_BLOCKS
        col = pl.multiple_of((s % N_LRU_BLOCKS) * LRU_BLOCK, LRU_BLOCK)
        out = []
        for j in range(SUBLANES):
            hbm = (u_hbm if inbound else h_hbm).at[bb, pl.ds(j * CHUNK, CHUNK), pl.ds(col, LRU_BLOCK)]
            if inbound:
                out.append(pltpu.make_async_copy(hbm, uin_ref.at[sl, pl.ds(j * PITCH, CHUNK), :], in_sem.at[sl]))
            else:
                out.append(pltpu.make_async_copy(hout_ref.at[sl, pl.ds(j * PITCH, CHUNK), :], hbm, out_sem.at[sl]))
        return out

    @pl.when(step == 0)
    def _():
        for cp in chunk_copies(step, slot, True):
            cp.start()

    @pl.when(step + 1 < n_steps)
    def _():
        for cp in chunk_copies(step + 1, 1 - slot, True):
            cp.start()

    for cp in chunk_copies(step, slot, True):
        cp.wait()

    row = lax.broadcasted_iota(jnp.int32, (SUBLANES, LRU_BLOCK), 0)

    def tile_rows(k):
        return pl.ds(pl.multiple_of(k * SUBLANES, SUBLANES), SUBLANES)

    def permute(k, c):
        up_ref[tile_rows(k + CONV_BEFORE), :] = uin_ref[slot, pl.ds(k, SUBLANES, stride=PITCH), :]
        return c

    lax.fori_loop(0, CHUNK, permute, 0, unroll=8)
    for i in range(CONV_BEFORE):
        src = up_ref[(CHUNK + i) * SUBLANES:(CHUNK + i + 1) * SUBLANES, :]
        up_ref[i * SUBLANES:(i + 1) * SUBLANES, :] = _shift_rows(src, row, True)
    for i in range(CONV_AFTER):
        src = up_ref[(CONV_BEFORE + i) * SUBLANES:(CONV_BEFORE + i + 1) * SUBLANES, :]
        dst = (CONV_BEFORE + CHUNK + i) * SUBLANES
        up_ref[dst:dst + SUBLANES, :] = _shift_rows(src, row, False)

    uc = cb_ref[0]
    for tap in range(CONV_BEFORE + CONV_AFTER + 1):
        uc = uc + up_ref[tap * SUBLANES:tap * SUBLANES + SEQ, :] * cw_ref[0, tap:tap + 1, :]
    uc_ref[...] = uc
    g_ref[...] = _dot(uc.astype(BF16), wg_ref[0, 0]) + bg_ref[0, 0]

    sp = jax.nn.softplus(ap_ref[0, 0])
    u_half = 0.5 * uc_ref[...]
    for d in range(2):
        quarter_coef = (-0.25 * RG_LRU_C) * sp[:, d * LRU_BLOCK:(d + 1) * LRU_BLOCK]
        g_r = g_ref[:, 2 * d * LRU_BLOCK:(2 * d + 1) * LRU_BLOCK]
        g_i = g_ref[:, (2 * d + 1) * LRU_BLOCK:(2 * d + 2) * LRU_BLOCK]
        x = quarter_coef + quarter_coef * jnp.tanh(g_r)
        a = jnp.exp2(TWO_LOG2E * x)
        s = -jnp.tanh(x)
        mult = (1.0 + a) * jnp.where(s > 0.0, s * lax.rsqrt(s), 0.0)
        gated = u_half + u_half * jnp.tanh(g_i)
        a_ref[d] = a
        b_ref[d] = mult * gated
        first = slice(0, SUBLANES) if d == 0 else slice(SEQ - SUBLANES, SEQ)
        first_row = 0 if d == 0 else SUBLANES - 1
        g_first = u_half[first, :] + u_half[first, :] * jnp.tanh(g_i[first, :])
        b_ref[d, first, :] = jnp.where(row == first_row, g_first, b_ref[d, first, :])

    def scan(k, c):
        hf, af, hb, ab = c
        rf = tile_rows(k)
        rb = tile_rows(CHUNK - 1 - k)
        a = a_ref[0, rf, :]
        hf = a * hf + b_ref[0, rf, :]
        af = a * af
        h_ref[0, rf, :] = hf
        p_ref[0, rf, :] = af
        a = a_ref[1, rb, :]
        hb = a * hb + b_ref[1, rb, :]
        ab = a * ab
        h_ref[1, rb, :] = hb
        p_ref[1, rb, :] = ab
        return hf, af, hb, ab

    zeros = jnp.zeros((SUBLANES, LRU_BLOCK), F32)
    ones = jnp.ones((SUBLANES, LRU_BLOCK), F32)
    hf, af, hb, ab = lax.fori_loop(0, CHUNK, scan, (zeros, ones, zeros, ones), unroll=4)
    carry_f = _shift_rows(_scan_sublanes(af, hf, row, False), row, True)
    carry_b = _shift_rows(_scan_sublanes(ab, hb, row, True), row, False)

    @pl.when(step >= 2)
    def _():
        for cp in chunk_copies(step, slot, False):
            cp.wait()

    def emit(k, c):
        r = tile_rows(k)
        h = (h_ref[0, r, :] + p_ref[0, r, :] * carry_f) + (h_ref[1, r, :] + p_ref[1, r, :] * carry_b)
        hout_ref[slot, pl.ds(k, SUBLANES, stride=PITCH), :] = h
        return c

    lax.fori_loop(0, CHUNK, emit, 0, unroll=8)
    for cp in chunk_copies(step, slot, False):
        cp.start()

    @pl.when(step == n_steps - 1)
    def _():
        for cp in chunk_copies(step, 1 - slot, False) + chunk_copies(step, slot, False):
            cp.wait()


def _lru_branch(l, u, conv_w, conv_b, wg, bg, a_param):
    batch = u.shape[0]
    vm = lambda *shape: pltpu.VMEM(shape, F32)
    stage = (2, SUBLANES * PITCH, LRU_BLOCK)
    return pl.pallas_call(
        _lru_kernel,
        grid=(batch, N_LRU_BLOCKS),
        in_specs=[
            pl.BlockSpec(memory_space=pl.ANY),
            pl.BlockSpec((1, 4, LRU_BLOCK), lambda b, n: (l, 0, n)),
            pl.BlockSpec((1, 1, LRU_BLOCK), lambda b, n: (l, 0, n)),
            pl.BlockSpec((1, 1, LRU_BLOCK, 4 * LRU_BLOCK), lambda b, n: (l, n, 0, 0)),
            pl.BlockSpec((1, 1, 1, 4 * LRU_BLOCK), lambda b, n: (l, n, 0, 0)),
            pl.BlockSpec((1, 1, 1, 2 * LRU_BLOCK), lambda b, n: (l, n, 0, 0)),
        ],
        out_specs=pl.BlockSpec(memory_space=pl.ANY),
        out_shape=jax.ShapeDtypeStruct((batch, SEQ, LRU_W), F32),
        scratch_shapes=[vm(*stage), vm(*stage),
                        vm(SEQ + (CONV_BEFORE + CONV_AFTER) * SUBLANES, LRU_BLOCK), vm(SEQ, LRU_BLOCK),
                        vm(SEQ, 4 * LRU_BLOCK)] + [vm(2, SEQ, LRU_BLOCK)] * 4 + [
                        pltpu.SemaphoreType.DMA((2,)), pltpu.SemaphoreType.DMA((2,))],
        compiler_params=_params(("arbitrary", "arbitrary"), 40),
        name="rg_lru",
    )(u, conv_w, conv_b, wg, bg, a_param)


def _gelu_tanh(x):
    return 0.5 * x * (1.0 + jnp.tanh(0.7978845608028654 * (x + 0.044715 * x * x * x)))


def _mix_kernel(*refs, moe):
    if moe:
        (h_ref, y_ref, at_ref, gb_ref, x_ref, g1_ref, sh_ref, sc_ref, n2_ref, wol_ref, wom_ref, wout_ref,
         wrh_ref, wrl_ref, xo_ref, h2_ref, lg_ref) = refs
    else:
        (h_ref, y_ref, at_ref, gb_ref, x_ref, g1_ref, sh_ref, sc_ref, n2_ref, wol_ref, wom_ref, wout_ref,
         xo_ref, h2_ref) = refs
    hg = (h_ref[...] * _gelu_tanh(y_ref[...].astype(F32))).astype(BF16)
    y_lru = _dot(hg, wol_ref[0])
    y_mla = _dot(at_ref[...], wom_ref[0])
    g_lru = jax.nn.sigmoid(gb_ref[:, 0:D_MODEL].astype(F32))
    g_mla = jax.nn.sigmoid(gb_ref[:, D_MODEL:2 * D_MODEL].astype(F32))
    z = (g_lru * y_lru + g_mla * y_mla).astype(BF16)
    xn = x_ref[...] + g1_ref[0] * _dot(z, wout_ref[0])
    xo_ref[...] = xn
    h2 = _rms(xn, 1.0 / D_MODEL) * n2_ref[0]
    h2 = h2 * (1.0 + sc_ref[0]) + sh_ref[0]
    if moe:
        h2_ref[...] = h2
        hh, hl = _split_bf16(h2)
        lg_ref[...] = _dot(hh, wrh_ref[0]) + (_dot(hh, wrl_ref[0]) + _dot(hl, wrh_ref[0]))
    else:
        h2_ref[...] = h2.astype(BF16)


def _mix(l, h_lru, y_gate, attn, gb, x, gate1, shift2, scale2, n2, wol, wom, wout, router=None):
    tm = TM_PROJ
    moe = router is not None
    row = lambda n: pl.BlockSpec((tm, n), lambda i: (i, 0))
    vec = _batch_vec_spec(tm)
    ins = [h_lru, y_gate, attn, gb, x, gate1, shift2, scale2, n2, wol, wom, wout]
    in_specs = [row(LRU_W), row(LRU_W), row(N_HEADS * V_DIM), row(2 * D_MODEL), row(D_MODEL), vec, vec, vec,
                _layer_spec(n2, l), _layer_spec(wol, l), _layer_spec(wom, l), _layer_spec(wout, l)]
    out_specs = [row(D_MODEL), row(D_MODEL)]
    out_shape = [jax.ShapeDtypeStruct((TOKENS, D_MODEL), F32),
                 jax.ShapeDtypeStruct((TOKENS, D_MODEL), F32 if moe else BF16)]
    if moe:
        ins += list(router)
        in_specs += [_layer_spec(router[0], l // 2), _layer_spec(router[1], l // 2)]
        out_specs.append(row(LANES))
        out_shape.append(jax.ShapeDtypeStruct((TOKENS, LANES), F32))
    return pl.pallas_call(
        functools.partial(_mix_kernel, moe=moe),
        grid=(TOKENS // tm,),
        in_specs=in_specs,
        out_specs=out_specs,
        out_shape=out_shape,
        compiler_params=_params(("arbitrary",), 48),
        name="mix_moe" if moe else "mix_dense",
    )(*ins)


def _swiglu_accumulate(h, wg, wu, wd, acc_ref, j):
    a = _dot(h, wg)
    b = _dot(h, wu)
    part = _dot((a * jax.nn.sigmoid(a) * b).astype(BF16), wd)

    @pl.when(j == 0)
    def _():
        acc_ref[...] = part

    @pl.when(j > 0)
    def _():
        acc_ref[...] += part


def _ffn_kernel(h_ref, x_ref, g2_ref, wg_ref, wu_ref, wd_ref, o_ref, acc_ref):
    j = pl.program_id(1)
    _swiglu_accumulate(h_ref[...], wg_ref[0], wu_ref[0], wd_ref[0], acc_ref, j)

    @pl.when(j == pl.num_programs(1) - 1)
    def _():
        o_ref[...] = x_ref[...] + g2_ref[0] * acc_ref[...]


def _dense_ffn(m, h2, x, gate2, wg, wu, wd):
    tm = TM_FFN
    per_batch = SEQ // tm
    row = pl.BlockSpec((tm, D_MODEL), lambda i, j: (i, 0))
    return pl.pallas_call(
        _ffn_kernel,
        grid=(TOKENS // tm, D_FF // TF),
        in_specs=[row, row, pl.BlockSpec((1, 1, D_MODEL), lambda i, j: (i // per_batch, 0, 0)),
                  pl.BlockSpec((1, D_MODEL, TF), lambda i, j: (m, 0, j)),
                  pl.BlockSpec((1, D_MODEL, TF), lambda i, j: (m, 0, j)),
                  pl.BlockSpec((1, TF, D_MODEL), lambda i, j: (m, j, 0))],
        out_specs=row,
        out_shape=jax.ShapeDtypeStruct((TOKENS, D_MODEL), F32),
        scratch_shapes=[pltpu.VMEM((tm, D_MODEL), F32)],
        compiler_params=_params(("arbitrary", "arbitrary"), 56),
        name="ffn_dense",
    )(h2, x, gate2, wg, wu, wd)


def _route_kernel(lg_ref, meta_ref, wt_ref, cnt_ref, carry_ref):
    i = pl.program_id(0)

    @pl.when(i == 0)
    def _():
        carry_ref[...] = jnp.zeros_like(carry_ref)

    tm = TM_ROUTE
    lane = lax.broadcasted_iota(jnp.int32, (tm, LANES), 1)
    lane_f = lane.astype(F32)
    neg = jnp.float32(-jnp.inf)
    lg = jnp.where(lane < N_EXPERTS, lg_ref[...], neg)
    m1 = jnp.max(lg, axis=-1, keepdims=True)
    i1 = jnp.min(jnp.where(lg == m1, lane_f, float(LANES)), axis=-1, keepdims=True).astype(jnp.int32)
    lg2 = jnp.where(lane == i1, neg, lg)
    m2 = jnp.max(lg2, axis=-1, keepdims=True)
    i2 = jnp.min(jnp.where(lg2 == m2, lane_f, float(LANES)), axis=-1, keepdims=True).astype(jnp.int32)
    e = jnp.exp(m2 - m1)
    w1 = 1.0 / (1.0 + e)
    w2 = e * w1
    sel = jnp.where(lane == i1, 1.0, jnp.where(lane == i2, 1.0, 0.0))
    r_i = lax.broadcasted_iota(jnp.int32, (tm, tm), 0)
    c_i = lax.broadcasted_iota(jnp.int32, (tm, tm), 1)
    tri = jnp.where(r_i > c_i, 1.0, 0.0).astype(BF16)
    cum = _dot(tri, sel.astype(BF16)) + carry_ref[0:1, :]
    r1 = jnp.sum(jnp.where(lane == i1, cum, 0.0), axis=-1, keepdims=True).astype(jnp.int32)
    r2 = jnp.sum(jnp.where(lane == i2, cum, 0.0), axis=-1, keepdims=True).astype(jnp.int32)
    total = carry_ref[0:1, :] + jnp.sum(sel, axis=0, keepdims=True)
    carry_ref[...] = jnp.broadcast_to(total, carry_ref.shape)
    cnt_ref[...] = jnp.broadcast_to(total, cnt_ref.shape)
    meta_ref[...] = jnp.where(lane == 0, i1, jnp.where(lane == 1, i2, jnp.where(lane == 2, r1,
                              jnp.where(lane == 3, r2, 0))))
    wt_ref[...] = jnp.where(lane == 0, w1, jnp.where(lane == 1, w2, 0.0))


def _route(logits):
    tm = TM_ROUTE
    row = pl.BlockSpec((tm, LANES), lambda i: (i, 0))
    cnt = pl.BlockSpec((SUBLANES, LANES), lambda i: (0, 0))
    return pl.pallas_call(
        _route_kernel,
        grid=(TOKENS // tm,),
        in_specs=[row],
        out_specs=[row, row, cnt],
        out_shape=[jax.ShapeDtypeStruct((TOKENS, LANES), jnp.int32),
                   jax.ShapeDtypeStruct((TOKENS, LANES), F32),
                   jax.ShapeDtypeStruct((SUBLANES, LANES), F32)],
        scratch_shapes=[pltpu.VMEM((SUBLANES, LANES), F32)],
        compiler_params=_params(("arbitrary",), 32),
        name="route_top2",
    )(logits)


def _row_copy(src_ref, src_row, dst_ref, dst_row, sem):
    return pltpu.make_async_copy(src_ref.at[pl.ds(src_row, 1)], dst_ref.at[pl.ds(dst_row, 1)], sem)


def _dispatch_kernel(p0_ref, p1_ref, h_ref, init_ref, xs_ref, sem):
    del init_ref
    tm = TM_DISPATCH
    base = pl.program_id(0) * tm

    def issue(g, c):
        for j in range(DMA_UNROLL):
            r = g * DMA_UNROLL + j
            _row_copy(h_ref, r, xs_ref, p0_ref[base + r], sem).start(priority=j % 2)
            _row_copy(h_ref, r, xs_ref, p1_ref[base + r], sem).start(priority=(j + 1) % 2)
        return c

    lax.fori_loop(0, tm // DMA_UNROLL, issue, 0)
    for _ in range(2):
        pltpu.make_async_copy(h_ref, xs_ref.at[pl.ds(0, tm)], sem).wait()


def _dispatch(pos0, pos1, h2):
    tm = TM_DISPATCH
    grid_spec = pltpu.PrefetchScalarGridSpec(
        num_scalar_prefetch=2,
        grid=(TOKENS // tm,),
        in_specs=[pl.BlockSpec((tm, D_MODEL), lambda i, p0, p1: (i, 0)),
                  pl.BlockSpec(memory_space=pl.ANY)],
        out_specs=pl.BlockSpec(memory_space=pl.ANY),
        scratch_shapes=[pltpu.SemaphoreType.DMA(())],
    )
    return pl.pallas_call(
        _dispatch_kernel,
        grid_spec=grid_spec,
        out_shape=jax.ShapeDtypeStruct((MOE_ROWS, D_MODEL), F32),
        input_output_aliases={3: 0},
        compiler_params=_params(("arbitrary",), 32),
        name="moe_dispatch",
    )(pos0, pos1, h2, jnp.zeros((MOE_ROWS, D_MODEL), F32))


def _moe_ffn_kernel(te_ref, valid_ref, x_ref, wg_ref, wu_ref, wd_ref, o_ref, acc_ref):
    i = pl.program_id(0)
    j = pl.program_id(1)
    last = j == pl.num_programs(1) - 1

    @pl.when(valid_ref[i] == 1)
    def _():
        _swiglu_accumulate(x_ref[...].astype(BF16), wg_ref[0, 0], wu_ref[0, 0], wd_ref[0, 0], acc_ref, j)

        @pl.when(last)
        def _():
            o_ref[...] = acc_ref[...]

    @pl.when(jnp.logical_and(valid_ref[i] == 0, last))
    def _():
        o_ref[...] = jnp.zeros_like(o_ref)


def _moe_ffn(m, tile_expert, tile_valid, xs, wg, wu, wd):
    tm = TM_FFN
    row = pl.BlockSpec((tm, D_MODEL), lambda i, j, te, va: (i, 0))
    grid_spec = pltpu.PrefetchScalarGridSpec(
        num_scalar_prefetch=2,
        grid=(N_MOE_TILES, D_FF // TF),
        in_specs=[row,
                  pl.BlockSpec((1, 1, D_MODEL, TF), lambda i, j, te, va: (m, te[i], 0, j * va[i])),
                  pl.BlockSpec((1, 1, D_MODEL, TF), lambda i, j, te, va: (m, te[i], 0, j * va[i])),
                  pl.BlockSpec((1, 1, TF, D_MODEL), lambda i, j, te, va: (m, te[i], j * va[i], 0))],
        out_specs=row,
        scratch_shapes=[pltpu.VMEM((tm, D_MODEL), F32)],
    )
    return pl.pallas_call(
        _moe_ffn_kernel,
        grid_spec=grid_spec,
        out_shape=jax.ShapeDtypeStruct((MOE_ROWS, D_MODEL), F32),
        compiler_params=_params(("arbitrary", "arbitrary"), 56),
        name="ffn_moe",
    )(tile_expert, tile_valid, xs, wg, wu, wd)


def _moe_combine_kernel(p0_ref, p1_ref, x_ref, g2_ref, wt_ref, y_ref, o_ref, buf_ref, sem):
    tm = TM_COMBINE
    base = pl.program_id(0) * tm

    def issue(g, c):
        for j in range(DMA_UNROLL):
            r = g * DMA_UNROLL + j
            _row_copy(y_ref, p0_ref[base + r], buf_ref.at[0], r, sem).start(priority=j % 2)
            _row_copy(y_ref, p1_ref[base + r], buf_ref.at[1], r, sem).start(priority=(j + 1) % 2)
        return c

    lax.fori_loop(0, tm // DMA_UNROLL, issue, 0)
    for k in range(2):
        pltpu.make_async_copy(y_ref.at[pl.ds(0, tm)], buf_ref.at[k], sem).wait()
    w = wt_ref[...]
    f = w[:, 0:1] * buf_ref[0] + w[:, 1:2] * buf_ref[1]
    o_ref[...] = x_ref[...] + g2_ref[0] * f


def _moe_combine(pos0, pos1, x, gate2, wts, y):
    tm = TM_COMBINE
    per_batch = SEQ // tm
    row = pl.BlockSpec((tm, D_MODEL), lambda i, p0, p1: (i, 0))
    grid_spec = pltpu.PrefetchScalarGridSpec(
        num_scalar_prefetch=2,
        grid=(TOKENS // tm,),
        in_specs=[row,
                  pl.BlockSpec((1, 1, D_MODEL), lambda i, p0, p1: (i // per_batch, 0, 0)),
                  pl.BlockSpec((tm, LANES), lambda i, p0, p1: (i, 0)),
                  pl.BlockSpec(memory_space=pl.ANY)],
        out_specs=row,
        scratch_shapes=[pltpu.VMEM((2, tm, D_MODEL), F32), pltpu.SemaphoreType.DMA(())],
    )
    return pl.pallas_call(
        _moe_combine_kernel,
        grid_spec=grid_spec,
        out_shape=jax.ShapeDtypeStruct((TOKENS, D_MODEL), F32),
        compiler_params=_params(("arbitrary",), 32),
        name="moe_combine",
    )(pos0, pos1, x, gate2, wts, y)


def _prep_w_in(w_in):
    cq_ckv = w_in[..., 0:Q_RANK + KV_RANK]
    kr = w_in[..., Q_RANK + KV_RANK:Q_RANK + KV_RANK + ROPE]
    rest = w_in[..., Q_RANK + KV_RANK + ROPE:]
    x1, x2 = kr[..., :ROPE // 2], kr[..., ROPE // 2:]
    z = lambda n: jnp.zeros(kr.shape[:-1] + (n,), w_in.dtype)
    kr_a = jnp.concatenate([z(NOPE), x1, x2, z(HEAD_TILE - QK_DIM)], axis=-1)
    kr_b = jnp.concatenate([z(NOPE), -x2, x1, z(HEAD_TILE - QK_DIM)], axis=-1)
    return jnp.concatenate([cq_ckv, kr_a, kr_b, rest], axis=-1).astype(BF16)


def _prep_w_uq(w_uq):
    w = w_uq.reshape(DEPTH, Q_RANK, N_HEADS, QK_DIM)
    nope, x1, x2 = w[..., :NOPE], w[..., NOPE:NOPE + ROPE // 2], w[..., NOPE + ROPE // 2:]
    z = lambda n: jnp.zeros(w.shape[:-1] + (n,), w.dtype)
    qa = jnp.concatenate([nope, x1, x2, z(HEAD_TILE - QK_DIM)], axis=-1)
    qb = jnp.concatenate([z(NOPE), -x2, x1, z(HEAD_TILE - QK_DIM)], axis=-1)
    flat = lambda a: a.reshape(DEPTH, Q_RANK, N_HEADS * HEAD_TILE)
    return jnp.concatenate([flat(qa), flat(qb)], axis=-1).astype(BF16)


def _prep_w_ukv(w_ukv):
    w = w_ukv.reshape(DEPTH, KV_RANK, N_HEADS, NOPE + V_DIM)
    k_nope, v = w[..., :NOPE], w[..., NOPE:]
    zk = jnp.zeros(k_nope.shape[:-1] + (HEAD_TILE - NOPE,), w.dtype)
    ka = jnp.concatenate([k_nope, zk], axis=-1)
    zv = jnp.zeros_like(v)
    even = (jnp.arange(N_HEADS) % 2 == 0)[None, None, :, None]
    vp = jnp.concatenate([jnp.where(even, v, zv), jnp.where(even, zv, v)], axis=-1)
    flat = lambda a: a.reshape(DEPTH, KV_RANK, N_HEADS * HEAD_TILE)
    return jnp.concatenate([flat(ka), flat(vp)], axis=-1).astype(BF16)


def _head_gain(g, scale):
    pad = jnp.zeros((DEPTH, HEAD_TILE - QK_DIM), g.dtype)
    return (jnp.concatenate([g, pad], axis=-1) * scale).reshape(DEPTH, 1, HEAD_TILE)


def _rope_tables(positions):
    inv_freq = 1.0 / (10000.0 ** (jnp.arange(0, ROPE, 2, dtype=F32) / ROPE))
    ang = positions.astype(F32).reshape(TOKENS, 1) * inv_freq
    cos, sin = jnp.cos(ang), jnp.sin(ang)
    cos_t = jnp.concatenate([jnp.ones((TOKENS, NOPE), F32), cos, cos,
                             jnp.zeros((TOKENS, HEAD_TILE - QK_DIM), F32)], axis=-1)
    sin_t = jnp.concatenate([jnp.zeros((TOKENS, NOPE), F32), sin, sin,
                             jnp.zeros((TOKENS, HEAD_TILE - QK_DIM), F32)], axis=-1)
    return cos_t, sin_t


def _moe_plan(meta, cnt):
    e1, e2, r1, r2 = meta[:, 0], meta[:, 1], meta[:, 2], meta[:, 3]
    counts = cnt[0, :N_EXPERTS].astype(jnp.int32)
    padded = ((counts + TM_FFN - 1) // TM_FFN) * TM_FFN
    ends = jnp.cumsum(padded)
    starts = ends - padded
    experts = jnp.arange(N_EXPERTS, dtype=jnp.int32)
    start_of = lambda e: jnp.sum(jnp.where(e[:, None] == experts[None, :], starts[None, :], 0), axis=1)
    pos0 = start_of(e1) + r1
    pos1 = start_of(e2) + r2
    tile_start = jnp.arange(N_MOE_TILES, dtype=jnp.int32) * TM_FFN
    tile_expert = jnp.minimum(jnp.sum(tile_start[:, None] >= ends[None, :], axis=1), N_EXPERTS - 1)
    tile_valid = (tile_start < ends[-1]).astype(jnp.int32)
    return pos0.astype(jnp.int32), pos1.astype(jnp.int32), tile_expert.astype(jnp.int32), tile_valid


def kernel(x, c, positions, ada_w, ada_b, norm1_g, norm2_g, w_in, q_norm_g, kv_norm_g, w_uq, w_ukv, q_head_g,
           k_head_g, w_o_mla, conv_w, conv_b, lru_gate_w, lru_gate_b, lru_a_param, w_o_lru, w_out, ffn_w_gate,
           ffn_w_up, ffn_w_down, moe_router, moe_w_gate, moe_w_up, moe_w_down):
    cos_t, sin_t = _rope_tables(positions)
    mod = _modulation(c, ada_w, ada_b).reshape(DEPTH, BATCH, 6, 1, D_MODEL)

    vec = lambda a: a.reshape(DEPTH, 1, a.shape[-1])
    w_in_p = _prep_w_in(w_in)
    wq_p = _prep_w_uq(w_uq)
    wkv_p = _prep_w_ukv(w_ukv)
    hq = _head_gain(q_head_g, 0.5 * TWO_LOG2E * QK_DIM ** -0.5)
    hk = _head_gain(k_head_g, 1.0)
    wg_p = (0.5 * lru_gate_w).transpose(0, 3, 4, 1, 2, 5).reshape(
        DEPTH, N_LRU_BLOCKS, LRU_BLOCK, 4 * LRU_BLOCK).astype(BF16)
    bg_p = (0.5 * lru_gate_b).reshape(DEPTH, 2, 2, N_LRU_BLOCKS, LRU_BLOCK).transpose(0, 3, 1, 2, 4).reshape(
        DEPTH, N_LRU_BLOCKS, 1, 4 * LRU_BLOCK)
    ap_p = lru_a_param.reshape(DEPTH, 2, N_LRU_BLOCKS, LRU_BLOCK).transpose(0, 2, 1, 3).reshape(
        DEPTH, N_LRU_BLOCKS, 1, 2 * LRU_BLOCK)
    conv_b_p = conv_b.reshape(DEPTH, 1, LRU_W)
    wol, wom, wout = w_o_lru.astype(BF16), w_o_mla.astype(BF16), w_out.astype(BF16)
    fg, fu, fd = ffn_w_gate.astype(BF16), ffn_w_up.astype(BF16), ffn_w_down.astype(BF16)
    mg, mu, md = moe_w_gate.astype(BF16), moe_w_up.astype(BF16), moe_w_down.astype(BF16)
    router = jnp.pad(moe_router, ((0, 0), (0, 0), (0, LANES - N_EXPERTS)))
    router_hi = router.astype(BF16)
    router_lo = (router - router_hi.astype(F32)).astype(BF16)
    n1, n2, gq, gkv = vec(norm1_g), vec(norm2_g), vec(q_norm_g), vec(kv_norm_g)

    xt = x.reshape(TOKENS, D_MODEL)
    for l in range(DEPTH):
        shift1, scale1, gate1, shift2, scale2, gate2 = [mod[l, :, k] for k in range(6)]
        mla_in, u, y_gate, gb = _in_projection(l, xt, shift1, scale1, n1, w_in_p)
        q, k, v = _mla_prep(l, mla_in, cos_t, sin_t, gq, gkv, wq_p, wkv_p, hq, hk)
        attn = _attention(q, k, v).reshape(TOKENS, N_HEADS * V_DIM)
        h_lru = _lru_branch(l, u.reshape(BATCH, SEQ, LRU_W), conv_w, conv_b_p, wg_p, bg_p, ap_p)
        h_lru = h_lru.reshape(TOKENS, LRU_W)
        m = l // 2
        if l % 2 == 0:
            xt, h2 = _mix(l, h_lru, y_gate, attn, gb, xt, gate1, shift2, scale2, n2, wol, wom, wout)
            xt = _dense_ffn(m, h2, xt, gate2, fg, fu, fd)
        else:
            xt, h2, logits = _mix(l, h_lru, y_gate, attn, gb, xt, gate1, shift2, scale2, n2, wol, wom, wout,
                                  router=(router_hi, router_lo))
            meta, wts, cnt = _route(logits)
            pos0, pos1, tile_expert, tile_valid = _moe_plan(meta, cnt)
            xs = _dispatch(pos0, pos1, h2)
            y = _moe_ffn(m, tile_expert, tile_valid, xs, mg, mu, md)
            xt = _moe_combine(pos0, pos1, xt, gate2, wts, y)
    return xt.reshape(BATCH, SEQ, D_MODEL)
```

```python
import functools

import jax
import jax.numpy as jnp
from jax import lax
from jax.experimental import pallas as pl
from jax.experimental.pallas import tpu as pltpu

F32 = jnp.float32
BF16 = jnp.bfloat16

D_MODEL = 1024
BATCH = 8
SEQ = 2048
TOKENS = BATCH * SEQ
DEPTH = 4
N_HEADS = 8
NOPE = 64
ROPE = 32
QK_DIM = NOPE + ROPE
V_DIM = 64
Q_RANK = 256
KV_RANK = 256
LRU_W = D_MODEL
N_LRU_BLOCKS = 8
LRU_BLOCK = LRU_W // N_LRU_BLOCKS
RG_LRU_C = 8.0
D_FF = 2816
N_EXPERTS = 8
N_MOE_LAYERS = DEPTH // 2
EPS = 1e-6

LANES = 128
SUBLANES = 8
HEAD_TILE = LANES

MLA_COLS = Q_RANK + KV_RANK + 2 * HEAD_TILE
IN_COLS = MLA_COLS + 2 * LRU_W + 2 * D_MODEL

TM_PROJ = 512
TQ = 512
TM_FFN = 512
TF = D_FF // 2
N_MOE_TILES = (2 * TOKENS) // TM_FFN + N_EXPERTS
MOE_ROWS = N_MOE_TILES * TM_FFN
TM_ROUTE = 512
TM_DISPATCH = 512
TM_COMBINE = 256
DMA_UNROLL = 8

MIB = 1024 * 1024


def _params(sem, vmem_mib):
    return pltpu.CompilerParams(dimension_semantics=sem, vmem_limit_bytes=vmem_mib * MIB)


def _dot(a, b):
    return jnp.dot(a, b, preferred_element_type=F32)


def _split_bf16(a):
    hi = a.astype(BF16)
    lo = (a - hi.astype(F32)).astype(BF16)
    return hi, lo


def _rms(v, inv_n):
    return v * lax.rsqrt(jnp.sum(v * v, axis=-1, keepdims=True) * inv_n + EPS)


def _layer_spec(a, l):
    nd = a.ndim - 1
    return pl.BlockSpec((1,) + a.shape[1:], lambda i: (l,) + (0,) * nd)


TN_MOD = 1536


def _mod_kernel(c_ref, w_ref, b_ref, o_ref):
    c = c_ref[...]
    ch, cl = _split_bf16(c * jax.nn.sigmoid(c))
    wh, wl = _split_bf16(w_ref[0])
    o_ref[0] = _dot(ch, wh) + (_dot(ch, wl) + _dot(cl, wh)) + b_ref[0]


def _modulation(c, ada_w, ada_b):
    n = 6 * D_MODEL
    return pl.pallas_call(
        _mod_kernel,
        grid=(DEPTH, n // TN_MOD),
        in_specs=[
            pl.BlockSpec((BATCH, D_MODEL), lambda l, j: (0, 0)),
            pl.BlockSpec((1, D_MODEL, TN_MOD), lambda l, j: (l, 0, j)),
            pl.BlockSpec((1, 1, TN_MOD), lambda l, j: (l, 0, j)),
        ],
        out_specs=pl.BlockSpec((1, BATCH, TN_MOD), lambda l, j: (l, 0, j)),
        out_shape=jax.ShapeDtypeStruct((DEPTH, BATCH, n), F32),
        compiler_params=_params(("arbitrary", "arbitrary"), 40),
        name="adaln_mod",
    )(c, ada_w, ada_b.reshape(DEPTH, 1, n))


def _inproj_kernel(x_ref, sh_ref, sc_ref, g_ref, w_ref, mla_ref, u_ref, y_ref, gb_ref):
    x = x_ref[...]
    h = _rms(x, 1.0 / D_MODEL) * g_ref[0]
    h = (h * (1.0 + sc_ref[0]) + sh_ref[0]).astype(BF16)
    c0, c1, c2 = MLA_COLS, MLA_COLS + LRU_W, MLA_COLS + 2 * LRU_W
    mla_ref[...] = _dot(h, w_ref[0, :, 0:c0])
    u_ref[...] = _dot(h, w_ref[0, :, c0:c1])
    y_ref[...] = _dot(h, w_ref[0, :, c1:c2]).astype(BF16)
    gb_ref[...] = _dot(h, w_ref[0, :, c2:IN_COLS]).astype(BF16)


def _batch_vec_spec(tm):
    per_batch = SEQ // tm
    return pl.BlockSpec((1, 1, D_MODEL), lambda i: (i // per_batch, 0, 0))


def _in_projection(l, x, shift, scale, gain, w):
    tm = TM_PROJ
    row = lambda n: pl.BlockSpec((tm, n), lambda i: (i, 0))
    return pl.pallas_call(
        _inproj_kernel,
        grid=(TOKENS // tm,),
        in_specs=[row(D_MODEL), _batch_vec_spec(tm), _batch_vec_spec(tm), _layer_spec(gain, l),
                  _layer_spec(w, l)],
        out_specs=[row(MLA_COLS), row(LRU_W), row(LRU_W), row(2 * D_MODEL)],
        out_shape=[
            jax.ShapeDtypeStruct((TOKENS, MLA_COLS), F32),
            jax.ShapeDtypeStruct((TOKENS, LRU_W), F32),
            jax.ShapeDtypeStruct((TOKENS, LRU_W), BF16),
            jax.ShapeDtypeStruct((TOKENS, 2 * D_MODEL), BF16),
        ],
        compiler_params=_params(("arbitrary",), 56),
        name="in_proj",
    )(x, shift, scale, gain, w)


def _mla_prep_kernel(m_ref, cos_ref, sin_ref, gq_ref, gkv_ref, wq_ref, wkv_ref, hq_ref, hk_ref,
                     q_ref, k_ref, v_ref):
    cos = cos_ref[...]
    sin = sin_ref[...]
    cq = m_ref[:, 0:Q_RANK]
    ckv = m_ref[:, Q_RANK:Q_RANK + KV_RANK]
    kr_a = m_ref[:, Q_RANK + KV_RANK:Q_RANK + KV_RANK + HEAD_TILE]
    kr_b = m_ref[:, Q_RANK + KV_RANK + HEAD_TILE:MLA_COLS]
    cqn = (_rms(cq, 1.0 / Q_RANK) * gq_ref[0]).astype(BF16)
    ckvn = (_rms(ckv, 1.0 / KV_RANK) * gkv_ref[0]).astype(BF16)
    qq = _dot(cqn, wq_ref[0])
    kv = _dot(ckvn, wkv_ref[0])
    k_pe = kr_a * cos + kr_b * sin
    nq = N_HEADS * HEAD_TILE
    lane = lax.broadcasted_iota(jnp.int32, (TM_PROJ, HEAD_TILE), 1)
    for h in range(N_HEADS):
        lo, hi = h * HEAD_TILE, (h + 1) * HEAD_TILE
        qh = qq[:, lo:hi] * cos + qq[:, nq + lo:nq + hi] * sin
        q_ref[0, h] = (_rms(qh, 1.0 / QK_DIM) * hq_ref[0]).astype(BF16)
        kh = kv[:, lo:hi] + k_pe
        k_ref[0, h] = (_rms(kh, 1.0 / QK_DIM) * hk_ref[0]).astype(BF16)
        v_ref[0, h] = jnp.where(lane == V_DIM * (1 - h % 2), 1.0, kv[:, nq + lo:nq + hi]).astype(BF16)


def _mla_prep(l, mla_in, cos_t, sin_t, gq, gkv, wq, wkv, hq, hk):
    tm = TM_PROJ
    per_batch = SEQ // tm
    row = lambda n: pl.BlockSpec((tm, n), lambda i: (i, 0))
    head = pl.BlockSpec((1, N_HEADS, tm, HEAD_TILE), lambda i: (i // per_batch, 0, i % per_batch, 0))
    shp = jax.ShapeDtypeStruct((BATCH, N_HEADS, SEQ, HEAD_TILE), BF16)
    params = [gq, gkv, wq, wkv, hq, hk]
    return pl.pallas_call(
        _mla_prep_kernel,
        grid=(TOKENS // tm,),
        in_specs=[row(MLA_COLS), row(HEAD_TILE), row(HEAD_TILE)] + [_layer_spec(p, l) for p in params],
        out_specs=[head, head, head],
        out_shape=[shp, shp, shp],
        compiler_params=_params(("arbitrary",), 48),
        name="mla_prep",
    )(mla_in, cos_t, sin_t, *params)


def _attn_kernel(q_ref, k_ref, v_ref, o_ref):
    lane = lax.broadcasted_iota(jnp.int32, (TQ, LANES), 1)
    for pair in range(N_HEADS // 2):
        acc = None
        for j in range(2):
            h = 2 * pair + j
            s = lax.dot_general(q_ref[0, h], k_ref[0, h], (((1,), (1,)), ((), ())),
                                preferred_element_type=F32)
            p = jnp.exp2(s - jnp.max(s, axis=-1, keepdims=True)).astype(BF16)
            o = _dot(p, v_ref[0, h])
            ones_lane = V_DIM * (1 - j)
            inv = 1.0 / o[:, ones_lane:ones_lane + 1]
            mine = (lane < V_DIM) if j == 0 else (lane >= V_DIM)
            o = jnp.where(mine, o * inv, 0.0)
            acc = o if acc is None else acc + o
        o_ref[0, :, pair * LANES:(pair + 1) * LANES] = acc.astype(BF16)


def _attention(q, k, v):
    nq = SEQ // TQ
    kv_spec = pl.BlockSpec((1, N_HEADS, SEQ, HEAD_TILE), lambda b, i: (b, 0, 0, 0))
    return pl.pallas_call(
        _attn_kernel,
        grid=(BATCH, nq),
        in_specs=[pl.BlockSpec((1, N_HEADS, TQ, HEAD_TILE), lambda b, i: (b, 0, i, 0)), kv_spec, kv_spec],
        out_specs=pl.BlockSpec((1, TQ, N_HEADS * V_DIM), lambda b, i: (b, i, 0)),
        out_shape=jax.ShapeDtypeStruct((BATCH, SEQ, N_HEADS * V_DIM), BF16),
        compiler_params=_params(("arbitrary", "arbitrary"), 48),
        name="attention",
    )(q, k, v)


CHUNK = SEQ // SUBLANES
PITCH = CHUNK + 4
CONV_BEFORE, CONV_AFTER = 2, 1
TWO_LOG2E = 2.0 * 1.4426950408889634


def _shift_rows(v, row, down):
    if down:
        return jnp.where(row == 0, 0.0, pltpu.roll(v, 1, 0))
    return jnp.where(row == SUBLANES - 1, 0.0, pltpu.roll(v, SUBLANES - 1, 0))


def _scan_sublanes(a, b, row, reverse):
    for s in (1, 2, 4):
        shift = SUBLANES - s if reverse else s
        a_s = pltpu.roll(a, shift, 0)
        b_s = pltpu.roll(b, shift, 0)
        keep = (row < SUBLANES - s) if reverse else (row >= s)
        b = jnp.where(keep, a * b_s + b, b)
        a = jnp.where(keep, a * a_s, a)
    return b


def _lru_kernel(u_hbm, cw_ref, cb_ref, wg_ref, bg_ref, ap_ref, h_hbm,
                uin_ref, hout_ref, up_ref, uc_ref, g_ref, a_ref, b_ref, h_ref, p_ref, in_sem, out_sem):
    n_steps = pl.num_programs(0) * N_LRU_BLOCKS
    step = pl.program_id(0) * N_LRU_BLOCKS + pl.program_id(1)
    slot = step % 2

    def chunk_copies(s, sl, inbound):
        bb = s // N_LRU_BLOCKS
        col = pl.multiple_of((s % N_LRU_BLOCKS) * LRU_BLOCK, LRU_BLOCK)
        out = []
        for j in range(SUBLANES):
            hbm = (u_hbm if inbound else h_hbm).at[bb, pl.ds(j * CHUNK, CHUNK), pl.ds(col, LRU_BLOCK)]
            if inbound:
                out.append(pltpu.make_async_copy(hbm, uin_ref.at[sl, pl.ds(j * PITCH, CHUNK), :], in_sem.at[sl]))
            else:
                out.append(pltpu.make_async_copy(hout_ref.at[sl, pl.ds(j * PITCH, CHUNK), :], hbm, out_sem.at[sl]))
        return out

    @pl.when(step == 0)
    def _():
        for cp in chunk_copies(step, slot, True):
            cp.start()

    @pl.when(step + 1 < n_steps)
    def _():
        for cp in chunk_copies(step + 1, 1 - slot, True):
            cp.start()

    for cp in chunk_copies(step, slot, True):
        cp.wait()

    row = lax.broadcasted_iota(jnp.int32, (SUBLANES, LRU_BLOCK), 0)

    def tile_rows(k):
        return pl.ds(pl.multiple_of(k * SUBLANES, SUBLANES), SUBLANES)

    def permute(k, c):
        up_ref[tile_rows(k + CONV_BEFORE), :] = uin_ref[slot, pl.ds(k, SUBLANES, stride=PITCH), :]
        return c

    lax.fori_loop(0, CHUNK, permute, 0, unroll=8)
    for i in range(CONV_BEFORE):
        src = up_ref[(CHUNK + i) * SUBLANES:(CHUNK + i + 1) * SUBLANES, :]
        up_ref[i * SUBLANES:(i + 1) * SUBLANES, :] = _shift_rows(src, row, True)
    for i in range(CONV_AFTER):
        src = up_ref[(CONV_BEFORE + i) * SUBLANES:(CONV_BEFORE + i + 1) * SUBLANES, :]
        dst = (CONV_BEFORE + CHUNK + i) * SUBLANES
        up_ref[dst:dst + SUBLANES, :] = _shift_rows(src, row, False)

    uc = cb_ref[0]
    for tap in range(CONV_BEFORE + CONV_AFTER + 1):
        uc = uc + up_ref[tap * SUBLANES:tap * SUBLANES + SEQ, :] * cw_ref[0, tap:tap + 1, :]
    uc_ref[...] = uc
    g_ref[...] = _dot(uc.astype(BF16), wg_ref[0, 0]) + bg_ref[0, 0]

    sp = jax.nn.softplus(ap_ref[0, 0])
    u_half = 0.5 * uc_ref[...]
    for d in range(2):
        quarter_coef = (-0.25 * RG_LRU_C) * sp[:, d * LRU_BLOCK:(d + 1) * LRU_BLOCK]
        g_r = g_ref[:, 2 * d * LRU_BLOCK:(2 * d + 1) * LRU_BLOCK]
        g_i = g_ref[:, (2 * d + 1) * LRU_BLOCK:(2 * d + 2) * LRU_BLOCK]
        x = quarter_coef + quarter_coef * jnp.tanh(g_r)
        a = jnp.exp2(TWO_LOG2E * x)
        s = -jnp.tanh(x)
        mult = (1.0 + a) * jnp.where(s > 0.0, s * lax.rsqrt(s), 0.0)
        gated = u_half + u_half * jnp.tanh(g_i)
        a_ref[d] = a
        b_ref[d] = mult * gated
        first = slice(0, SUBLANES) if d == 0 else slice(SEQ - SUBLANES, SEQ)
        first_row = 0 if d == 0 else SUBLANES - 1
        g_first = u_half[first, :] + u_half[first, :] * jnp.tanh(g_i[first, :])
        b_ref[d, first, :] = jnp.where(row == first_row, g_first, b_ref[d, first, :])

    def scan(k, c):
        hf, af, hb, ab = c
        rf = tile_rows(k)
        rb = tile_rows(CHUNK - 1 - k)
        a = a_ref[0, rf, :]
        hf = a * hf + b_ref[0, rf, :]
        af = a * af
        h_ref[0, rf, :] = hf
        p_ref[0, rf, :] = af
        a = a_ref[1, rb, :]
        hb = a * hb + b_ref[1, rb, :]
        ab = a * ab
        h_ref[1, rb, :] = hb
        p_ref[1, rb, :] = ab
        return hf, af, hb, ab

    zeros = jnp.zeros((SUBLANES, LRU_BLOCK), F32)
    ones = jnp.ones((SUBLANES, LRU_BLOCK), F32)
    hf, af, hb, ab = lax.fori_loop(0, CHUNK, scan, (zeros, ones, zeros, ones), unroll=4)
    carry_f = _shift_rows(_scan_sublanes(af, hf, row, False), row, True)
    carry_b = _shift_rows(_scan_sublanes(ab, hb, row, True), row, False)

    @pl.when(step >= 2)
    def _():
        for cp in chunk_copies(step, slot, False):
            cp.wait()

    def emit(k, c):
        r = tile_rows(k)
        h = (h_ref[0, r, :] + p_ref[0, r, :] * carry_f) + (h_ref[1, r, :] + p_ref[1, r, :] * carry_b)
        hout_ref[slot, pl.ds(k, SUBLANES, stride=PITCH), :] = h
        return c

    lax.fori_loop(0, CHUNK, emit, 0, unroll=8)
    for cp in chunk_copies(step, slot, False):
        cp.start()

    @pl.when(step == n_steps - 1)
    def _():
        for cp in chunk_copies(step, 1 - slot, False) + chunk_copies(step, slot, False):
            cp.wait()


def _lru_branch(l, u, conv_w, conv_b, wg, bg, a_param):
    batch = u.shape[0]
    vm = lambda *shape: pltpu.VMEM(shape, F32)
    stage = (2, SUBLANES * PITCH, LRU_BLOCK)
    return pl.pallas_call(
        _lru_kernel,
        grid=(batch, N_LRU_BLOCKS),
        in_specs=[
            pl.BlockSpec(memory_space=pl.ANY),
            pl.BlockSpec((1, 4, LRU_BLOCK), lambda b, n: (l, 0, n)),
            pl.BlockSpec((1, 1, LRU_BLOCK), lambda b, n: (l, 0, n)),
            pl.BlockSpec((1, 1, LRU_BLOCK, 4 * LRU_BLOCK), lambda b, n: (l, n, 0, 0)),
            pl.BlockSpec((1, 1, 1, 4 * LRU_BLOCK), lambda b, n: (l, n, 0, 0)),
            pl.BlockSpec((1, 1, 1, 2 * LRU_BLOCK), lambda b, n: (l, n, 0, 0)),
        ],
        out_specs=pl.BlockSpec(memory_space=pl.ANY),
        out_shape=jax.ShapeDtypeStruct((batch, SEQ, LRU_W), F32),
        scratch_shapes=[vm(*stage), vm(*stage),
                        vm(SEQ + (CONV_BEFORE + CONV_AFTER) * SUBLANES, LRU_BLOCK), vm(SEQ, LRU_BLOCK),
                        vm(SEQ, 4 * LRU_BLOCK)] + [vm(2, SEQ, LRU_BLOCK)] * 4 + [
                        pltpu.SemaphoreType.DMA((2,)), pltpu.SemaphoreType.DMA((2,))],
        compiler_params=_params(("arbitrary", "arbitrary"), 40),
        name="rg_lru",
    )(u, conv_w, conv_b, wg, bg, a_param)


def _gelu_tanh(x):
    return 0.5 * x * (1.0 + jnp.tanh(0.7978845608028654 * (x + 0.044715 * x * x * x)))


def _mix_kernel(*refs, moe):
    if moe:
        (h_ref, y_ref, at_ref, gb_ref, x_ref, g1_ref, sh_ref, sc_ref, n2_ref, wol_ref, wom_ref, wout_ref,
         wr_ref, xo_ref, h2_ref, lg_ref) = refs
    else:
        (h_ref, y_ref, at_ref, gb_ref, x_ref, g1_ref, sh_ref, sc_ref, n2_ref, wol_ref, wom_ref, wout_ref,
         xo_ref, h2_ref) = refs
    hg = (h_ref[...] * _gelu_tanh(y_ref[...].astype(F32))).astype(BF16)
    y_lru = _dot(hg, wol_ref[0])
    y_mla = _dot(at_ref[...], wom_ref[0])
    g_lru = jax.nn.sigmoid(gb_ref[:, 0:D_MODEL].astype(F32))
    g_mla = jax.nn.sigmoid(gb_ref[:, D_MODEL:2 * D_MODEL].astype(F32))
    z = (g_lru * y_lru + g_mla * y_mla).astype(BF16)
    xn = x_ref[...] + g1_ref[0] * _dot(z, wout_ref[0])
    xo_ref[...] = xn
    h2 = _rms(xn, 1.0 / D_MODEL) * n2_ref[0]
    h2 = h2 * (1.0 + sc_ref[0]) + sh_ref[0]
    if moe:
        h2_ref[...] = h2
        hh, hl = _split_bf16(h2)
        parts = _dot(hh, wr_ref[0]) + _dot(hl, wr_ref[0])
        lg_ref[...] = parts + pltpu.roll(parts, LANES - N_EXPERTS, 1)
    else:
        h2_ref[...] = h2.astype(BF16)


def _mix(l, h_lru, y_gate, attn, gb, x, gate1, shift2, scale2, n2, wol, wom, wout, router=None):
    tm = TM_PROJ
    moe = router is not None
    row = lambda n: pl.BlockSpec((tm, n), lambda i: (i, 0))
    vec = _batch_vec_spec(tm)
    ins = [h_lru, y_gate, attn, gb, x, gate1, shift2, scale2, n2, wol, wom, wout]
    in_specs = [row(LRU_W), row(LRU_W), row(N_HEADS * V_DIM), row(2 * D_MODEL), row(D_MODEL), vec, vec, vec,
                _layer_spec(n2, l), _layer_spec(wol, l), _layer_spec(wom, l), _layer_spec(wout, l)]
    out_specs = [row(D_MODEL), row(D_MODEL)]
    out_shape = [jax.ShapeDtypeStruct((TOKENS, D_MODEL), F32),
                 jax.ShapeDtypeStruct((TOKENS, D_MODEL), F32 if moe else BF16)]
    if moe:
        ins.append(router)
        in_specs.append(_layer_spec(router, l // 2))
        out_specs.append(row(LANES))
        out_shape.append(jax.ShapeDtypeStruct((TOKENS, LANES), F32))
    return pl.pallas_call(
        functools.partial(_mix_kernel, moe=moe),
        grid=(TOKENS // tm,),
        in_specs=in_specs,
        out_specs=out_specs,
        out_shape=out_shape,
        compiler_params=_params(("arbitrary",), 48),
        name="mix_moe" if moe else "mix_dense",
    )(*ins)


def _swiglu_accumulate(h, wg, wu, wd, acc_ref, j):
    a = _dot(h, wg)
    b = _dot(h, wu)
    part = _dot((a * jax.nn.sigmoid(a) * b).astype(BF16), wd)

    @pl.when(j == 0)
    def _():
        acc_ref[...] = part

    @pl.when(j > 0)
    def _():
        acc_ref[...] += part


def _ffn_kernel(h_ref, x_ref, g2_ref, wg_ref, wu_ref, wd_ref, o_ref, acc_ref):
    j = pl.program_id(1)
    _swiglu_accumulate(h_ref[...], wg_ref[0], wu_ref[0], wd_ref[0], acc_ref, j)

    @pl.when(j == pl.num_programs(1) - 1)
    def _():
        o_ref[...] = x_ref[...] + g2_ref[0] * acc_ref[...]


def _dense_ffn(m, h2, x, gate2, wg, wu, wd):
    tm = TM_FFN
    per_batch = SEQ // tm
    row = pl.BlockSpec((tm, D_MODEL), lambda i, j: (i, 0))
    return pl.pallas_call(
        _ffn_kernel,
        grid=(TOKENS // tm, D_FF // TF),
        in_specs=[row, row, pl.BlockSpec((1, 1, D_MODEL), lambda i, j: (i // per_batch, 0, 0)),
                  pl.BlockSpec((1, D_MODEL, TF), lambda i, j: (m, 0, j)),
                  pl.BlockSpec((1, D_MODEL, TF), lambda i, j: (m, 0, j)),
                  pl.BlockSpec((1, TF, D_MODEL), lambda i, j: (m, j, 0))],
        out_specs=row,
        out_shape=jax.ShapeDtypeStruct((TOKENS, D_MODEL), F32),
        scratch_shapes=[pltpu.VMEM((tm, D_MODEL), F32)],
        compiler_params=_params(("arbitrary", "arbitrary"), 56),
        name="ffn_dense",
    )(h2, x, gate2, wg, wu, wd)


def _route_kernel(lg_ref, meta_ref, wt_ref, cnt_ref, carry_ref):
    i = pl.program_id(0)

    @pl.when(i == 0)
    def _():
        carry_ref[...] = jnp.zeros_like(carry_ref)

    tm = TM_ROUTE
    lane = lax.broadcasted_iota(jnp.int32, (tm, LANES), 1)
    lane_f = lane.astype(F32)
    neg = jnp.float32(-jnp.inf)
    lg = jnp.where(lane < N_EXPERTS, lg_ref[...], neg)
    m1 = jnp.max(lg, axis=-1, keepdims=True)
    i1 = jnp.min(jnp.where(lg == m1, lane_f, float(LANES)), axis=-1, keepdims=True).astype(jnp.int32)
    lg2 = jnp.where(lane == i1, neg, lg)
    m2 = jnp.max(lg2, axis=-1, keepdims=True)
    i2 = jnp.min(jnp.where(lg2 == m2, lane_f, float(LANES)), axis=-1, keepdims=True).astype(jnp.int32)
    e = jnp.exp(m2 - m1)
    w1 = 1.0 / (1.0 + e)
    w2 = e * w1
    sel = jnp.where(lane == i1, 1.0, jnp.where(lane == i2, 1.0, 0.0))
    r_i = lax.broadcasted_iota(jnp.int32, (tm, tm), 0)
    c_i = lax.broadcasted_iota(jnp.int32, (tm, tm), 1)
    tri = jnp.where(r_i > c_i, 1.0, 0.0).astype(BF16)
    cum = _dot(tri, sel.astype(BF16)) + carry_ref[0:1, :]
    r1 = jnp.sum(jnp.where(lane == i1, cum, 0.0), axis=-1, keepdims=True).astype(jnp.int32)
    r2 = jnp.sum(jnp.where(lane == i2, cum, 0.0), axis=-1, keepdims=True).astype(jnp.int32)
    total = carry_ref[0:1, :] + jnp.sum(sel, axis=0, keepdims=True)
    carry_ref[...] = jnp.broadcast_to(total, carry_ref.shape)
    cnt_ref[...] = jnp.broadcast_to(total, cnt_ref.shape)
    meta_ref[...] = jnp.where(lane == 0, i1, jnp.where(lane == 1, i2, jnp.where(lane == 2, r1,
                              jnp.where(lane == 3, r2, 0))))
    wt_ref[...] = jnp.where(lane == 0, w1, jnp.where(lane == 1, w2, 0.0))


def _route(logits):
    tm = TM_ROUTE
    row = pl.BlockSpec((tm, LANES), lambda i: (i, 0))
    cnt = pl.BlockSpec((SUBLANES, LANES), lambda i: (0, 0))
    return pl.pallas_call(
        _route_kernel,
        grid=(TOKENS // tm,),
        in_specs=[row],
        out_specs=[row, row, cnt],
        out_shape=[jax.ShapeDtypeStruct((TOKENS, LANES), jnp.int32),
                   jax.ShapeDtypeStruct((TOKENS, LANES), F32),
                   jax.ShapeDtypeStruct((SUBLANES, LANES), F32)],
        scratch_shapes=[pltpu.VMEM((SUBLANES, LANES), F32)],
        compiler_params=_params(("arbitrary",), 32),
        name="route_top2",
    )(logits)


def _row_copy(src_ref, src_row, dst_ref, dst_row, sem):
    return pltpu.make_async_copy(src_ref.at[pl.ds(src_row, 1)], dst_ref.at[pl.ds(dst_row, 1)], sem)


def _dispatch_kernel(p0_ref, p1_ref, h_hbm, init_ref, xs_ref, buf_ref, load_sem, row_sem):
    del init_ref
    tm = TM_DISPATCH
    step = pl.program_id(0)
    n_steps = pl.num_programs(0)
    slot = step % 2
    base = step * tm

    def load(tile, sl):
        return pltpu.make_async_copy(h_hbm.at[pl.ds(tile * tm, tm)], buf_ref.at[sl], load_sem.at[sl])

    def drain_rows():
        for _ in range(2):
            pltpu.make_async_copy(buf_ref.at[0], xs_ref.at[pl.ds(0, tm)], row_sem).wait()

    @pl.when(step == 0)
    def _():
        load(step, slot).start()

    @pl.when(step > 0)
    def _():
        drain_rows()

    @pl.when(step + 1 < n_steps)
    def _():
        load(step + 1, 1 - slot).start()

    load(step, slot).wait()
    src = buf_ref.at[slot]

    def issue(g, c):
        for j in range(DMA_UNROLL):
            r = g * DMA_UNROLL + j
            _row_copy(src, r, xs_ref, p0_ref[base + r], row_sem).start(priority=j % 2)
            _row_copy(src, r, xs_ref, p1_ref[base + r], row_sem).start(priority=(j + 1) % 2)
        return c

    lax.fori_loop(0, tm // DMA_UNROLL, issue, 0)

    @pl.when(step == n_steps - 1)
    def _():
        drain_rows()


def _dispatch(pos0, pos1, h2):
    tm = TM_DISPATCH
    grid_spec = pltpu.PrefetchScalarGridSpec(
        num_scalar_prefetch=2,
        grid=(TOKENS // tm,),
        in_specs=[pl.BlockSpec(memory_space=pl.ANY),
                  pl.BlockSpec(memory_space=pl.ANY)],
        out_specs=pl.BlockSpec(memory_space=pl.ANY),
        scratch_shapes=[pltpu.VMEM((2, tm, D_MODEL), F32), pltpu.SemaphoreType.DMA((2,)),
                        pltpu.SemaphoreType.DMA(())],
    )
    return pl.pallas_call(
        _dispatch_kernel,
        grid_spec=grid_spec,
        out_shape=jax.ShapeDtypeStruct((MOE_ROWS, D_MODEL), F32),
        input_output_aliases={3: 0},
        compiler_params=_params(("arbitrary",), 32),
        name="moe_dispatch",
    )(pos0, pos1, h2, jnp.zeros((MOE_ROWS, D_MODEL), F32))


def _moe_ffn_kernel(te_ref, valid_ref, x_ref, wg_ref, wu_ref, wd_ref, o_ref, acc_ref):
    i = pl.program_id(0)
    j = pl.program_id(1)
    last = j == pl.num_programs(1) - 1

    @pl.when(valid_ref[i] == 1)
    def _():
        _swiglu_accumulate(x_ref[...].astype(BF16), wg_ref[0, 0], wu_ref[0, 0], wd_ref[0, 0], acc_ref, j)

        @pl.when(last)
        def _():
            o_ref[...] = acc_ref[...]

    @pl.when(jnp.logical_and(valid_ref[i] == 0, last))
    def _():
        o_ref[...] = jnp.zeros_like(o_ref)


def _moe_ffn(m, tile_expert, tile_valid, xs, wg, wu, wd):
    tm = TM_FFN
    row = pl.BlockSpec((tm, D_MODEL), lambda i, j, te, va: (i, 0))
    grid_spec = pltpu.PrefetchScalarGridSpec(
        num_scalar_prefetch=2,
        grid=(N_MOE_TILES, D_FF // TF),
        in_specs=[row,
                  pl.BlockSpec((1, 1, D_MODEL, TF), lambda i, j, te, va: (m, te[i], 0, j * va[i])),
                  pl.BlockSpec((1, 1, D_MODEL, TF), lambda i, j, te, va: (m, te[i], 0, j * va[i])),
                  pl.BlockSpec((1, 1, TF, D_MODEL), lambda i, j, te, va: (m, te[i], j * va[i], 0))],
        out_specs=row,
        scratch_shapes=[pltpu.VMEM((tm, D_MODEL), F32)],
    )
    return pl.pallas_call(
        _moe_ffn_kernel,
        grid_spec=grid_spec,
        out_shape=jax.ShapeDtypeStruct((MOE_ROWS, D_MODEL), F32),
        compiler_params=_params(("arbitrary", "arbitrary"), 56),
        name="ffn_moe",
    )(tile_expert, tile_valid, xs, wg, wu, wd)


def _moe_combine_kernel(p0_ref, p1_ref, x_ref, g2_ref, wt_ref, y_ref, o_ref, buf_ref, sem):
    tm = TM_COMBINE
    step = pl.program_id(0)
    slot = step % 2

    def gather(tile, sl):
        base = tile * tm

        def issue(g, c):
            for j in range(DMA_UNROLL):
                r = g * DMA_UNROLL + j
                _row_copy(y_ref, p0_ref[base + r], buf_ref.at[sl, 0], r, sem.at[sl]).start(priority=j % 2)
                _row_copy(y_ref, p1_ref[base + r], buf_ref.at[sl, 1], r, sem.at[sl]).start(priority=(j + 1) % 2)
            return c

        lax.fori_loop(0, tm // DMA_UNROLL, issue, 0)

    @pl.when(step == 0)
    def _():
        gather(step, slot)

    @pl.when(step + 1 < pl.num_programs(0))
    def _():
        gather(step + 1, 1 - slot)

    for k in range(2):
        pltpu.make_async_copy(y_ref.at[pl.ds(0, tm)], buf_ref.at[slot, k], sem.at[slot]).wait()
    w = wt_ref[...]
    f = w[:, 0:1] * buf_ref[slot, 0] + w[:, 1:2] * buf_ref[slot, 1]
    o_ref[...] = x_ref[...] + g2_ref[0] * f


def _moe_combine(pos0, pos1, x, gate2, wts, y):
    tm = TM_COMBINE
    per_batch = SEQ // tm
    row = pl.BlockSpec((tm, D_MODEL), lambda i, p0, p1: (i, 0))
    grid_spec = pltpu.PrefetchScalarGridSpec(
        num_scalar_prefetch=2,
        grid=(TOKENS // tm,),
        in_specs=[row,
                  pl.BlockSpec((1, 1, D_MODEL), lambda i, p0, p1: (i // per_batch, 0, 0)),
                  pl.BlockSpec((tm, LANES), lambda i, p0, p1: (i, 0)),
                  pl.BlockSpec(memory_space=pl.ANY)],
        out_specs=row,
        scratch_shapes=[pltpu.VMEM((2, 2, tm, D_MODEL), F32), pltpu.SemaphoreType.DMA((2,))],
    )
    return pl.pallas_call(
        _moe_combine_kernel,
        grid_spec=grid_spec,
        out_shape=jax.ShapeDtypeStruct((TOKENS, D_MODEL), F32),
        compiler_params=_params(("arbitrary",), 32),
        name="moe_combine",
    )(pos0, pos1, x, gate2, wts, y)


def _prep_w_in(w_in):
    cq_ckv = w_in[..., 0:Q_RANK + KV_RANK]
    kr = w_in[..., Q_RANK + KV_RANK:Q_RANK + KV_RANK + ROPE]
    rest = w_in[..., Q_RANK + KV_RANK + ROPE:]
    x1, x2 = kr[..., :ROPE // 2], kr[..., ROPE // 2:]
    z = lambda n: jnp.zeros(kr.shape[:-1] + (n,), w_in.dtype)
    kr_a = jnp.concatenate([z(NOPE), x1, x2, z(HEAD_TILE - QK_DIM)], axis=-1)
    kr_b = jnp.concatenate([z(NOPE), -x2, x1, z(HEAD_TILE - QK_DIM)], axis=-1)
    return jnp.concatenate([cq_ckv, kr_a, kr_b, rest], axis=-1).astype(BF16)


def _prep_w_uq(w_uq):
    w = w_uq.reshape(DEPTH, Q_RANK, N_HEADS, QK_DIM)
    nope, x1, x2 = w[..., :NOPE], w[..., NOPE:NOPE + ROPE // 2], w[..., NOPE + ROPE // 2:]
    z = lambda n: jnp.zeros(w.shape[:-1] + (n,), w.dtype)
    qa = jnp.concatenate([nope, x1, x2, z(HEAD_TILE - QK_DIM)], axis=-1)
    qb = jnp.concatenate([z(NOPE), -x2, x1, z(HEAD_TILE - QK_DIM)], axis=-1)
    flat = lambda a: a.reshape(DEPTH, Q_RANK, N_HEADS * HEAD_TILE)
    return jnp.concatenate([flat(qa), flat(qb)], axis=-1).astype(BF16)


def _prep_w_ukv(w_ukv):
    w = w_ukv.reshape(DEPTH, KV_RANK, N_HEADS, NOPE + V_DIM)
    k_nope, v = w[..., :NOPE], w[..., NOPE:]
    zk = jnp.zeros(k_nope.shape[:-1] + (HEAD_TILE - NOPE,), w.dtype)
    ka = jnp.concatenate([k_nope, zk], axis=-1)
    zv = jnp.zeros_like(v)
    even = (jnp.arange(N_HEADS) % 2 == 0)[None, None, :, None]
    vp = jnp.concatenate([jnp.where(even, v, zv), jnp.where(even, zv, v)], axis=-1)
    flat = lambda a: a.reshape(DEPTH, KV_RANK, N_HEADS * HEAD_TILE)
    return jnp.concatenate([flat(ka), flat(vp)], axis=-1).astype(BF16)


def _head_gain(g, scale):
    pad = jnp.zeros((DEPTH, HEAD_TILE - QK_DIM), g.dtype)
    return (jnp.concatenate([g, pad], axis=-1) * scale).reshape(DEPTH, 1, HEAD_TILE)


def _rope_tables(positions):
    inv_freq = 1.0 / (10000.0 ** (jnp.arange(0, ROPE, 2, dtype=F32) / ROPE))
    ang = positions.astype(F32).reshape(TOKENS, 1) * inv_freq
    cos, sin = jnp.cos(ang), jnp.sin(ang)
    cos_t = jnp.concatenate([jnp.ones((TOKENS, NOPE), F32), cos, cos,
                             jnp.zeros((TOKENS, HEAD_TILE - QK_DIM), F32)], axis=-1)
    sin_t = jnp.concatenate([jnp.zeros((TOKENS, NOPE), F32), sin, sin,
                             jnp.zeros((TOKENS, HEAD_TILE - QK_DIM), F32)], axis=-1)
    return cos_t, sin_t


def _moe_plan(meta, cnt):
    e1, e2, r1, r2 = meta[:, 0], meta[:, 1], meta[:, 2], meta[:, 3]
    counts = cnt[0, :N_EXPERTS].astype(jnp.int32)
    padded = ((counts + TM_FFN - 1) // TM_FFN) * TM_FFN
    ends = jnp.cumsum(padded)
    starts = ends - padded
    experts = jnp.arange(N_EXPERTS, dtype=jnp.int32)
    start_of = lambda e: jnp.sum(jnp.where(e[:, None] == experts[None, :], starts[None, :], 0), axis=1)
    pos0 = start_of(e1) + r1
    pos1 = start_of(e2) + r2
    tile_start = jnp.arange(N_MOE_TILES, dtype=jnp.int32) * TM_FFN
    tile_expert = jnp.minimum(jnp.sum(tile_start[:, None] >= ends[None, :], axis=1), N_EXPERTS - 1)
    tile_valid = (tile_start < ends[-1]).astype(jnp.int32)
    return pos0.astype(jnp.int32), pos1.astype(jnp.int32), tile_expert.astype(jnp.int32), tile_valid


def kernel(x, c, positions, ada_w, ada_b, norm1_g, norm2_g, w_in, q_norm_g, kv_norm_g, w_uq, w_ukv, q_head_g,
           k_head_g, w_o_mla, conv_w, conv_b, lru_gate_w, lru_gate_b, lru_a_param, w_o_lru, w_out, ffn_w_gate,
           ffn_w_up, ffn_w_down, moe_router, moe_w_gate, moe_w_up, moe_w_down):
    cos_t, sin_t = _rope_tables(positions)
    mod = _modulation(c, ada_w, ada_b).reshape(DEPTH, BATCH, 6, 1, D_MODEL)

    vec = lambda a: a.reshape(DEPTH, 1, a.shape[-1])
    w_in_p = _prep_w_in(w_in)
    wq_p = _prep_w_uq(w_uq)
    wkv_p = _prep_w_ukv(w_ukv)
    hq = _head_gain(q_head_g, 0.5 * TWO_LOG2E * QK_DIM ** -0.5)
    hk = _head_gain(k_head_g, 1.0)
    wg_p = (0.5 * lru_gate_w).transpose(0, 3, 4, 1, 2, 5).reshape(
        DEPTH, N_LRU_BLOCKS, LRU_BLOCK, 4 * LRU_BLOCK).astype(BF16)
    bg_p = (0.5 * lru_gate_b).reshape(DEPTH, 2, 2, N_LRU_BLOCKS, LRU_BLOCK).transpose(0, 3, 1, 2, 4).reshape(
        DEPTH, N_LRU_BLOCKS, 1, 4 * LRU_BLOCK)
    ap_p = lru_a_param.reshape(DEPTH, 2, N_LRU_BLOCKS, LRU_BLOCK).transpose(0, 2, 1, 3).reshape(
        DEPTH, N_LRU_BLOCKS, 1, 2 * LRU_BLOCK)
    conv_b_p = conv_b.reshape(DEPTH, 1, LRU_W)
    wol, wom, wout = w_o_lru.astype(BF16), w_o_mla.astype(BF16), w_out.astype(BF16)
    fg, fu, fd = ffn_w_gate.astype(BF16), ffn_w_up.astype(BF16), ffn_w_down.astype(BF16)
    mg, mu, md = moe_w_gate.astype(BF16), moe_w_up.astype(BF16), moe_w_down.astype(BF16)
    router_hi = moe_router.astype(BF16)
    router_lo = (moe_router - router_hi.astype(F32)).astype(BF16)
    router = jnp.pad(jnp.concatenate([router_hi, router_lo], axis=-1),
                     ((0, 0), (0, 0), (0, LANES - 2 * N_EXPERTS)))
    n1, n2, gq, gkv = vec(norm1_g), vec(norm2_g), vec(q_norm_g), vec(kv_norm_g)

    xt = x.reshape(TOKENS, D_MODEL)
    for l in range(DEPTH):
        shift1, scale1, gate1, shift2, scale2, gate2 = [mod[l, :, k] for k in range(6)]
        mla_in, u, y_gate, gb = _in_projection(l, xt, shift1, scale1, n1, w_in_p)
        q, k, v = _mla_prep(l, mla_in, cos_t, sin_t, gq, gkv, wq_p, wkv_p, hq, hk)
        attn = _attention(q, k, v).reshape(TOKENS, N_HEADS * V_DIM)
        h_lru = _lru_branch(l, u.reshape(BATCH, SEQ, LRU_W), conv_w, conv_b_p, wg_p, bg_p, ap_p)
        h_lru = h_lru.reshape(TOKENS, LRU_W)
        m = l // 2
        if l % 2 == 0:
            xt, h2 = _mix(l, h_lru, y_gate, attn, gb, xt, gate1, shift2, scale2, n2, wol, wom, wout)
            xt = _dense_ffn(m, h2, xt, gate2, fg, fu, fd)
        else:
            xt, h2, logits = _mix(l, h_lru, y_gate, attn, gb, xt, gate1, shift2, scale2, n2, wol, wom, wout,
                                  router=router)
            meta, wts, cnt = _route(logits)
            pos0, pos1, tile_expert, tile_valid = _moe_plan(meta, cnt)
            xs = _dispatch(pos0, pos1, h2)
            y = _moe_ffn(m, tile_expert, tile_valid, xs, mg, mu, md)
            xt = _moe_combine(pos0, pos1, xt, gate2, wts, y)
    return xt.reshape(BATCH, SEQ, D_MODEL)
```

```python
import functools

import jax
import jax.numpy as jnp
from jax import lax
from jax.experimental import pallas as pl
from jax.experimental.pallas import tpu as pltpu

F32 = jnp.float32
BF16 = jnp.bfloat16

D_MODEL = 1024
BATCH = 8
SEQ = 2048
TOKENS = BATCH * SEQ
DEPTH = 4
N_HEADS = 8
NOPE = 64
ROPE = 32
QK_DIM = NOPE + ROPE
V_DIM = 64
Q_RANK = 256
KV_RANK = 256
LRU_W = D_MODEL
N_LRU_BLOCKS = 8
LRU_BLOCK = LRU_W // N_LRU_BLOCKS
RG_LRU_C = 8.0
D_FF = 2816
N_EXPERTS = 8
N_MOE_LAYERS = DEPTH // 2
EPS = 1e-6

LANES = 128
SUBLANES = 8
HEAD_TILE = LANES

MLA_COLS = Q_RANK + KV_RANK + 2 * HEAD_TILE
IN_COLS = MLA_COLS + 2 * LRU_W + 2 * D_MODEL

TM_PROJ = 512
TQ = 512
TM_FFN = 512
TF = D_FF // 2
N_MOE_TILES = (2 * TOKENS) // TM_FFN + N_EXPERTS
MOE_ROWS = N_MOE_TILES * TM_FFN
TM_DISPATCH = 512
TM_COMBINE = 256

MIB = 1024 * 1024


def _params(sem, vmem_mib):
    return pltpu.CompilerParams(dimension_semantics=sem, vmem_limit_bytes=vmem_mib * MIB)


def _dot(a, b):
    return jnp.dot(a, b, preferred_element_type=F32)


def _split_bf16(a):
    hi = a.astype(BF16)
    lo = (a - hi.astype(F32)).astype(BF16)
    return hi, lo


def _rms(v, inv_n):
    return v * lax.rsqrt(jnp.sum(v * v, axis=-1, keepdims=True) * inv_n + EPS)


def _layer_spec(a, l):
    nd = a.ndim - 1
    return pl.BlockSpec((1,) + a.shape[1:], lambda i: (l,) + (0,) * nd)


TN_MOD = 1536


def _mod_kernel(c_ref, w_ref, b_ref, o_ref):
    c = c_ref[...]
    ch, cl = _split_bf16(c * jax.nn.sigmoid(c))
    wh, wl = _split_bf16(w_ref[0])
    o_ref[0] = _dot(ch, wh) + (_dot(ch, wl) + _dot(cl, wh)) + b_ref[0]


def _modulation(c, ada_w, ada_b):
    n = 6 * D_MODEL
    return pl.pallas_call(
        _mod_kernel,
        grid=(DEPTH, n // TN_MOD),
        in_specs=[
            pl.BlockSpec((BATCH, D_MODEL), lambda l, j: (0, 0)),
            pl.BlockSpec((1, D_MODEL, TN_MOD), lambda l, j: (l, 0, j)),
            pl.BlockSpec((1, 1, TN_MOD), lambda l, j: (l, 0, j)),
        ],
        out_specs=pl.BlockSpec((1, BATCH, TN_MOD), lambda l, j: (l, 0, j)),
        out_shape=jax.ShapeDtypeStruct((DEPTH, BATCH, n), F32),
        compiler_params=_params(("arbitrary", "arbitrary"), 40),
        name="adaln_mod",
    )(c, ada_w, ada_b.reshape(DEPTH, 1, n))


def _batch_vec_spec(tm):
    per_batch = SEQ // tm
    return pl.BlockSpec((1, 1, D_MODEL), lambda i: (i // per_batch, 0, 0))


def _resident_spec(a, l):
    nd = a.ndim - 1
    return pl.BlockSpec((1,) + a.shape[1:], lambda i: (l,) + (0,) * nd, pipeline_mode=pl.Buffered(1))


def _inproj_kernel(x_ref, sh_ref, sc_ref, g_ref, w_ref, cos_ref, sin_ref, gq_ref, gkv_ref, wq_ref, wkv_ref,
                   hq_ref, hk_ref, q_ref, k_ref, v_ref, u_ref, y_ref, gb_ref):
    x = x_ref[...]
    h = _rms(x, 1.0 / D_MODEL) * g_ref[0]
    h = (h * (1.0 + sc_ref[0]) + sh_ref[0]).astype(BF16)
    c0, c1, c2 = MLA_COLS, MLA_COLS + LRU_W, MLA_COLS + 2 * LRU_W
    m = _dot(h, w_ref[0, :, 0:c0])
    u_ref[...] = _dot(h, w_ref[0, :, c0:c1])
    y_ref[...] = _dot(h, w_ref[0, :, c1:c2]).astype(BF16)
    gb_ref[...] = _dot(h, w_ref[0, :, c2:IN_COLS]).astype(BF16)

    cos = cos_ref[...]
    sin = sin_ref[...]
    cq = m[:, 0:Q_RANK]
    ckv = m[:, Q_RANK:Q_RANK + KV_RANK]
    kr_a = m[:, Q_RANK + KV_RANK:Q_RANK + KV_RANK + HEAD_TILE]
    kr_b = m[:, Q_RANK + KV_RANK + HEAD_TILE:MLA_COLS]
    cqn = (_rms(cq, 1.0 / Q_RANK) * gq_ref[0]).astype(BF16)
    ckvn = (_rms(ckv, 1.0 / KV_RANK) * gkv_ref[0]).astype(BF16)
    qq = _dot(cqn, wq_ref[0])
    kv = _dot(ckvn, wkv_ref[0])
    k_pe = kr_a * cos + kr_b * sin
    nq = N_HEADS * HEAD_TILE
    lane = lax.broadcasted_iota(jnp.int32, (TM_PROJ, HEAD_TILE), 1)
    for hd in range(N_HEADS):
        lo, hi = hd * HEAD_TILE, (hd + 1) * HEAD_TILE
        qh = qq[:, lo:hi] * cos + qq[:, nq + lo:nq + hi] * sin
        q_ref[0, hd] = (_rms(qh, 1.0 / QK_DIM) * hq_ref[0]).astype(BF16)
        kh = kv[:, lo:hi] + k_pe
        k_ref[0, hd] = (_rms(kh, 1.0 / QK_DIM) * hk_ref[0]).astype(BF16)
        v_ref[0, hd] = jnp.where(lane == V_DIM * (1 - hd % 2), 1.0, kv[:, nq + lo:nq + hi]).astype(BF16)


def _in_projection(l, x, shift, scale, gain, w, cos_t, sin_t, gq, gkv, wq, wkv, hq, hk):
    tm = TM_PROJ
    per_batch = SEQ // tm
    row = lambda n: pl.BlockSpec((tm, n), lambda i: (i, 0))
    head = pl.BlockSpec((1, N_HEADS, tm, HEAD_TILE), lambda i: (i // per_batch, 0, i % per_batch, 0))
    head_shape = jax.ShapeDtypeStruct((BATCH, N_HEADS, SEQ, HEAD_TILE), BF16)
    params = [gq, gkv, wq, wkv, hq, hk]
    return pl.pallas_call(
        _inproj_kernel,
        grid=(TOKENS // tm,),
        in_specs=[row(D_MODEL), _batch_vec_spec(tm), _batch_vec_spec(tm), _layer_spec(gain, l),
                  _resident_spec(w, l), row(HEAD_TILE), row(HEAD_TILE)] + [_resident_spec(p, l) for p in params],
        out_specs=[head, head, head, row(LRU_W), row(LRU_W), row(2 * D_MODEL)],
        out_shape=[
            head_shape, head_shape, head_shape,
            jax.ShapeDtypeStruct((TOKENS, LRU_W), F32),
            jax.ShapeDtypeStruct((TOKENS, LRU_W), BF16),
            jax.ShapeDtypeStruct((TOKENS, 2 * D_MODEL), BF16),
        ],
        compiler_params=_params(("arbitrary",), 56),
        name="in_proj",
    )(x, shift, scale, gain, w, cos_t, sin_t, *params)


def _attn_kernel(q_ref, k_ref, v_ref, o_ref):
    lane = lax.broadcasted_iota(jnp.int32, (TQ, LANES), 1)
    for pair in range(N_HEADS // 2):
        acc = None
        for j in range(2):
            h = 2 * pair + j
            s = lax.dot_general(q_ref[0, h], k_ref[0, h], (((1,), (1,)), ((), ())),
                                preferred_element_type=F32)
            p = jnp.exp2(s - jnp.max(s, axis=-1, keepdims=True)).astype(BF16)
            o = _dot(p, v_ref[0, h])
            ones_lane = V_DIM * (1 - j)
            inv = 1.0 / o[:, ones_lane:ones_lane + 1]
            mine = (lane < V_DIM) if j == 0 else (lane >= V_DIM)
            o = jnp.where(mine, o * inv, 0.0)
            acc = o if acc is None else acc + o
        o_ref[0, :, pair * LANES:(pair + 1) * LANES] = acc.astype(BF16)


def _attention(q, k, v):
    nq = SEQ // TQ
    kv_spec = pl.BlockSpec((1, N_HEADS, SEQ, HEAD_TILE), lambda b, i: (b, 0, 0, 0))
    return pl.pallas_call(
        _attn_kernel,
        grid=(BATCH, nq),
        in_specs=[pl.BlockSpec((1, N_HEADS, TQ, HEAD_TILE), lambda b, i: (b, 0, i, 0)), kv_spec, kv_spec],
        out_specs=pl.BlockSpec((1, TQ, N_HEADS * V_DIM), lambda b, i: (b, i, 0)),
        out_shape=jax.ShapeDtypeStruct((BATCH, SEQ, N_HEADS * V_DIM), BF16),
        compiler_params=_params(("arbitrary", "arbitrary"), 48),
        name="attention",
    )(q, k, v)


CHUNK = SEQ // SUBLANES
PITCH = CHUNK + 4
CONV_BEFORE, CONV_AFTER = 2, 1
TWO_LOG2E = 2.0 * 1.4426950408889634


def _shift_rows(v, row, down):
    if down:
        return jnp.where(row == 0, 0.0, pltpu.roll(v, 1, 0))
    return jnp.where(row == SUBLANES - 1, 0.0, pltpu.roll(v, SUBLANES - 1, 0))


def _scan_sublanes(a, b, row, reverse):
    for s in (1, 2, 4):
        shift = SUBLANES - s if reverse else s
        a_s = pltpu.roll(a, shift, 0)
        b_s = pltpu.roll(b, shift, 0)
        keep = (row < SUBLANES - s) if reverse else (row >= s)
        b = jnp.where(keep, a * b_s + b, b)
        a = jnp.where(keep, a * a_s, a)
    return b


def _lru_kernel(u_hbm, cw_ref, cb_ref, wg_ref, bg_ref, ap_ref, h_hbm,
                uin_ref, hout_ref, up_ref, uc_ref, g_ref, a_ref, b_ref, h_ref, p_ref, in_sem, out_sem):
    n_steps = pl.num_programs(0) * N_LRU_BLOCKS
    step = pl.program_id(0) * N_LRU_BLOCKS + pl.program_id(1)
    slot = step % 2

    def chunk_copies(s, sl, inbound):
        bb = s // N_LRU_BLOCKS
        col = pl.multiple_of((s % N_LRU_BLOCKS) * LRU_BLOCK, LRU_BLOCK)
        out = []
        for j in range(SUBLANES):
            hbm = (u_hbm if inbound else h_hbm).at[bb, pl.ds(j * CHUNK, CHUNK), pl.ds(col, LRU_BLOCK)]
            if inbound:
                out.append(pltpu.make_async_copy(hbm, uin_ref.at[sl, pl.ds(j * PITCH, CHUNK), :], in_sem.at[sl]))
            else:
                out.append(pltpu.make_async_copy(hout_ref.at[sl, pl.ds(j * PITCH, CHUNK), :], hbm, out_sem.at[sl]))
        return out

    @pl.when(step == 0)
    def _():
        for cp in chunk_copies(step, slot, True):
            cp.start()

    @pl.when(step + 1 < n_steps)
    def _():
        for cp in chunk_copies(step + 1, 1 - slot, True):
            cp.start()

    for cp in chunk_copies(step, slot, True):
        cp.wait()

    row = lax.broadcasted_iota(jnp.int32, (SUBLANES, LRU_BLOCK), 0)

    def tile_rows(k):
        return pl.ds(pl.multiple_of(k * SUBLANES, SUBLANES), SUBLANES)

    def permute(k, c):
        up_ref[tile_rows(k + CONV_BEFORE), :] = uin_ref[slot, pl.ds(k, SUBLANES, stride=PITCH), :]
        return c

    lax.fori_loop(0, CHUNK, permute, 0, unroll=8)
    for i in range(CONV_BEFORE):
        src = up_ref[(CHUNK + i) * SUBLANES:(CHUNK + i + 1) * SUBLANES, :]
        up_ref[i * SUBLANES:(i + 1) * SUBLANES, :] = _shift_rows(src, row, True)
    for i in range(CONV_AFTER):
        src = up_ref[(CONV_BEFORE + i) * SUBLANES:(CONV_BEFORE + i + 1) * SUBLANES, :]
        dst = (CONV_BEFORE + CHUNK + i) * SUBLANES
        up_ref[dst:dst + SUBLANES, :] = _shift_rows(src, row, False)

    uc = cb_ref[0]
    for tap in range(CONV_BEFORE + CONV_AFTER + 1):
        uc = uc + up_ref[tap * SUBLANES:tap * SUBLANES + SEQ, :] * cw_ref[0, tap:tap + 1, :]
    uc_ref[...] = uc
    g_ref[...] = _dot(uc.astype(BF16), wg_ref[0, 0]) + bg_ref[0, 0]

    sp = jax.nn.softplus(ap_ref[0, 0])
    u_half = 0.5 * uc_ref[...]
    for d in range(2):
        quarter_coef = (-0.25 * RG_LRU_C) * sp[:, d * LRU_BLOCK:(d + 1) * LRU_BLOCK]
        g_r = g_ref[:, 2 * d * LRU_BLOCK:(2 * d + 1) * LRU_BLOCK]
        g_i = g_ref[:, (2 * d + 1) * LRU_BLOCK:(2 * d + 2) * LRU_BLOCK]
        x = quarter_coef + quarter_coef * jnp.tanh(g_r)
        a = jnp.exp2(TWO_LOG2E * x)
        s = -jnp.tanh(x)
        mult = (1.0 + a) * jnp.where(s > 0.0, s * lax.rsqrt(s), 0.0)
        gated = u_half + u_half * jnp.tanh(g_i)
        a_ref[d] = a
        b_ref[d] = mult * gated
        first = slice(0, SUBLANES) if d == 0 else slice(SEQ - SUBLANES, SEQ)
        first_row = 0 if d == 0 else SUBLANES - 1
        g_first = u_half[first, :] + u_half[first, :] * jnp.tanh(g_i[first, :])
        b_ref[d, first, :] = jnp.where(row == first_row, g_first, b_ref[d, first, :])

    def scan(k, c):
        hf, af, hb, ab = c
        rf = tile_rows(k)
        rb = tile_rows(CHUNK - 1 - k)
        a = a_ref[0, rf, :]
        hf = a * hf + b_ref[0, rf, :]
        af = a * af
        h_ref[0, rf, :] = hf
        p_ref[0, rf, :] = af
        a = a_ref[1, rb, :]
        hb = a * hb + b_ref[1, rb, :]
        ab = a * ab
        h_ref[1, rb, :] = hb
        p_ref[1, rb, :] = ab
        return hf, af, hb, ab

    zeros = jnp.zeros((SUBLANES, LRU_BLOCK), F32)
    ones = jnp.ones((SUBLANES, LRU_BLOCK), F32)
    hf, af, hb, ab = lax.fori_loop(0, CHUNK, scan, (zeros, ones, zeros, ones), unroll=4)
    carry_f = _shift_rows(_scan_sublanes(af, hf, row, False), row, True)
    carry_b = _shift_rows(_scan_sublanes(ab, hb, row, True), row, False)

    @pl.when(step >= 2)
    def _():
        for cp in chunk_copies(step, slot, False):
            cp.wait()

    def emit(k, c):
        r = tile_rows(k)
        h = (h_ref[0, r, :] + p_ref[0, r, :] * carry_f) + (h_ref[1, r, :] + p_ref[1, r, :] * carry_b)
        hout_ref[slot, pl.ds(k, SUBLANES, stride=PITCH), :] = h
        return c

    lax.fori_loop(0, CHUNK, emit, 0, unroll=8)
    for cp in chunk_copies(step, slot, False):
        cp.start()

    @pl.when(step == n_steps - 1)
    def _():
        for cp in chunk_copies(step, 1 - slot, False) + chunk_copies(step, slot, False):
            cp.wait()


def _lru_branch(l, u, conv_w, conv_b, wg, bg, a_param):
    batch = u.shape[0]
    vm = lambda *shape: pltpu.VMEM(shape, F32)
    stage = (2, SUBLANES * PITCH, LRU_BLOCK)
    return pl.pallas_call(
        _lru_kernel,
        grid=(batch, N_LRU_BLOCKS),
        in_specs=[
            pl.BlockSpec(memory_space=pl.ANY),
            pl.BlockSpec((1, 4, LRU_BLOCK), lambda b, n: (l, 0, n)),
            pl.BlockSpec((1, 1, LRU_BLOCK), lambda b, n: (l, 0, n)),
            pl.BlockSpec((1, 1, LRU_BLOCK, 4 * LRU_BLOCK), lambda b, n: (l, n, 0, 0)),
            pl.BlockSpec((1, 1, 1, 4 * LRU_BLOCK), lambda b, n: (l, n, 0, 0)),
            pl.BlockSpec((1, 1, 1, 2 * LRU_BLOCK), lambda b, n: (l, n, 0, 0)),
        ],
        out_specs=pl.BlockSpec(memory_space=pl.ANY),
        out_shape=jax.ShapeDtypeStruct((batch, SEQ, LRU_W), F32),
        scratch_shapes=[vm(*stage), vm(*stage),
                        vm(SEQ + (CONV_BEFORE + CONV_AFTER) * SUBLANES, LRU_BLOCK), vm(SEQ, LRU_BLOCK),
                        vm(SEQ, 4 * LRU_BLOCK)] + [vm(2, SEQ, LRU_BLOCK)] * 4 + [
                        pltpu.SemaphoreType.DMA((2,)), pltpu.SemaphoreType.DMA((2,))],
        compiler_params=_params(("arbitrary", "arbitrary"), 40),
        name="rg_lru",
    )(u, conv_w, conv_b, wg, bg, a_param)


def _gelu_tanh(x):
    return 0.5 * x * (1.0 + jnp.tanh(0.7978845608028654 * (x + 0.044715 * x * x * x)))


def _route(lg, meta_ref, wt_ref, cnt_ref, carry_ref):
    tm = lg.shape[0]

    @pl.when(pl.program_id(0) == 0)
    def _():
        carry_ref[...] = jnp.zeros_like(carry_ref)

    lane = lax.broadcasted_iota(jnp.int32, (tm, LANES), 1)
    lane_f = lane.astype(F32)
    neg = jnp.float32(-jnp.inf)
    lg = jnp.where(lane < N_EXPERTS, lg, neg)
    m1 = jnp.max(lg, axis=-1, keepdims=True)
    i1 = jnp.min(jnp.where(lg == m1, lane_f, float(LANES)), axis=-1, keepdims=True).astype(jnp.int32)
    lg2 = jnp.where(lane == i1, neg, lg)
    m2 = jnp.max(lg2, axis=-1, keepdims=True)
    i2 = jnp.min(jnp.where(lg2 == m2, lane_f, float(LANES)), axis=-1, keepdims=True).astype(jnp.int32)
    e = jnp.exp(m2 - m1)
    w1 = 1.0 / (1.0 + e)
    w2 = e * w1
    sel = jnp.where(lane == i1, 1.0, jnp.where(lane == i2, 1.0, 0.0))
    r_i = lax.broadcasted_iota(jnp.int32, (tm, tm), 0)
    c_i = lax.broadcasted_iota(jnp.int32, (tm, tm), 1)
    tri = jnp.where(r_i > c_i, 1.0, 0.0).astype(BF16)
    cum = _dot(tri, sel.astype(BF16)) + carry_ref[0:1, :]
    r1 = jnp.sum(jnp.where(lane == i1, cum, 0.0), axis=-1, keepdims=True).astype(jnp.int32)
    r2 = jnp.sum(jnp.where(lane == i2, cum, 0.0), axis=-1, keepdims=True).astype(jnp.int32)
    total = carry_ref[0:1, :] + jnp.sum(sel, axis=0, keepdims=True)
    carry_ref[...] = jnp.broadcast_to(total, carry_ref.shape)
    cnt_ref[...] = jnp.broadcast_to(total, cnt_ref.shape)
    meta_ref[...] = jnp.where(lane == 0, i1, jnp.where(lane == 1, i2, jnp.where(lane == 2, r1,
                              jnp.where(lane == 3, r2, 0))))
    wt_ref[...] = jnp.where(lane == 0, w1, jnp.where(lane == 1, w2, 0.0))


def _mix_kernel(*refs, moe):
    if moe:
        (h_ref, y_ref, at_ref, gb_ref, x_ref, g1_ref, sh_ref, sc_ref, n2_ref, wol_ref, wom_ref, wout_ref,
         wr_ref, xo_ref, h2_ref, meta_ref, wt_ref, cnt_ref, carry_ref) = refs
    else:
        (h_ref, y_ref, at_ref, gb_ref, x_ref, g1_ref, sh_ref, sc_ref, n2_ref, wol_ref, wom_ref, wout_ref,
         xo_ref, h2_ref) = refs
    hg = (h_ref[...] * _gelu_tanh(y_ref[...].astype(F32))).astype(BF16)
    y_lru = _dot(hg, wol_ref[0])
    y_mla = _dot(at_ref[...], wom_ref[0])
    g_lru = jax.nn.sigmoid(gb_ref[:, 0:D_MODEL].astype(F32))
    g_mla = jax.nn.sigmoid(gb_ref[:, D_MODEL:2 * D_MODEL].astype(F32))
    z = (g_lru * y_lru + g_mla * y_mla).astype(BF16)
    xn = x_ref[...] + g1_ref[0] * _dot(z, wout_ref[0])
    xo_ref[...] = xn
    h2 = _rms(xn, 1.0 / D_MODEL) * n2_ref[0]
    h2 = h2 * (1.0 + sc_ref[0]) + sh_ref[0]
    if moe:
        h2_ref[...] = h2
        hh, hl = _split_bf16(h2)
        parts = _dot(hh, wr_ref[0]) + _dot(hl, wr_ref[0])
        _route(parts + pltpu.roll(parts, LANES - N_EXPERTS, 1), meta_ref, wt_ref, cnt_ref, carry_ref)
    else:
        h2_ref[...] = h2.astype(BF16)


def _mix(l, h_lru, y_gate, attn, gb, x, gate1, shift2, scale2, n2, wol, wom, wout, router=None):
    tm = TM_PROJ
    moe = router is not None
    row = lambda n: pl.BlockSpec((tm, n), lambda i: (i, 0))
    vec = _batch_vec_spec(tm)
    ins = [h_lru, y_gate, attn, gb, x, gate1, shift2, scale2, n2, wol, wom, wout]
    in_specs = [row(LRU_W), row(LRU_W), row(N_HEADS * V_DIM), row(2 * D_MODEL), row(D_MODEL), vec, vec, vec,
                _layer_spec(n2, l), _layer_spec(wol, l), _layer_spec(wom, l), _layer_spec(wout, l)]
    out_specs = [row(D_MODEL), row(D_MODEL)]
    out_shape = [jax.ShapeDtypeStruct((TOKENS, D_MODEL), F32),
                 jax.ShapeDtypeStruct((TOKENS, D_MODEL), F32 if moe else BF16)]
    scratch = []
    if moe:
        ins.append(router)
        in_specs.append(_layer_spec(router, l // 2))
        out_specs += [row(LANES), row(LANES), pl.BlockSpec((SUBLANES, LANES), lambda i: (0, 0))]
        out_shape += [jax.ShapeDtypeStruct((TOKENS, LANES), jnp.int32),
                      jax.ShapeDtypeStruct((TOKENS, LANES), F32),
                      jax.ShapeDtypeStruct((SUBLANES, LANES), F32)]
        scratch.append(pltpu.VMEM((SUBLANES, LANES), F32))
    return pl.pallas_call(
        functools.partial(_mix_kernel, moe=moe),
        grid=(TOKENS // tm,),
        in_specs=in_specs,
        out_specs=out_specs,
        out_shape=out_shape,
        scratch_shapes=scratch,
        compiler_params=_params(("arbitrary",), 48),
        name="mix_moe" if moe else "mix_dense",
    )(*ins)


def _swiglu_accumulate(h, wg, wu, wd, acc_ref, j):
    a = _dot(h, wg)
    b = _dot(h, wu)
    part = _dot((a * jax.nn.sigmoid(a) * b).astype(BF16), wd)

    @pl.when(j == 0)
    def _():
        acc_ref[...] = part

    @pl.when(j > 0)
    def _():
        acc_ref[...] += part


def _ffn_kernel(h_ref, x_ref, g2_ref, wg_ref, wu_ref, wd_ref, o_ref, acc_ref):
    j = pl.program_id(1)
    _swiglu_accumulate(h_ref[...], wg_ref[0], wu_ref[0], wd_ref[0], acc_ref, j)

    @pl.when(j == pl.num_programs(1) - 1)
    def _():
        o_ref[...] = x_ref[...] + g2_ref[0] * acc_ref[...]


def _dense_ffn(m, h2, x, gate2, wg, wu, wd):
    tm = TM_FFN
    per_batch = SEQ // tm
    row = pl.BlockSpec((tm, D_MODEL), lambda i, j: (i, 0))
    return pl.pallas_call(
        _ffn_kernel,
        grid=(TOKENS // tm, D_FF // TF),
        in_specs=[row, row, pl.BlockSpec((1, 1, D_MODEL), lambda i, j: (i // per_batch, 0, 0)),
                  pl.BlockSpec((1, D_MODEL, TF), lambda i, j: (m, 0, j)),
                  pl.BlockSpec((1, D_MODEL, TF), lambda i, j: (m, 0, j)),
                  pl.BlockSpec((1, TF, D_MODEL), lambda i, j: (m, j, 0))],
        out_specs=row,
        out_shape=jax.ShapeDtypeStruct((TOKENS, D_MODEL), F32),
        scratch_shapes=[pltpu.VMEM((tm, D_MODEL), F32)],
        compiler_params=_params(("arbitrary", "arbitrary"), 56),
        name="ffn_dense",
    )(h2, x, gate2, wg, wu, wd)


def _row_copy(src_ref, src_row, dst_ref, dst_row, sem):
    return pltpu.make_async_copy(src_ref.at[pl.ds(src_row, 1)], dst_ref.at[pl.ds(dst_row, 1)], sem)


def _dispatch_kernel(p0_ref, p1_ref, h_hbm, init_ref, xs_ref, buf_ref, load_sem, row_sem):
    del init_ref
    tm = TM_DISPATCH
    step = pl.program_id(0)
    n_steps = pl.num_programs(0)
    slot = step % 2
    base = step * tm

    def load(tile, sl):
        return pltpu.make_async_copy(h_hbm.at[pl.ds(tile * tm, tm)], buf_ref.at[sl], load_sem.at[sl])

    def drain_rows():
        for _ in range(2):
            pltpu.make_async_copy(buf_ref.at[0], xs_ref.at[pl.ds(0, tm)], row_sem).wait()

    @pl.when(step == 0)
    def _():
        load(step, slot).start()

    @pl.when(step > 0)
    def _():
        drain_rows()

    @pl.when(step + 1 < n_steps)
    def _():
        load(step + 1, 1 - slot).start()

    load(step, slot).wait()
    src = buf_ref.at[slot]

    for r in range(tm):
        _row_copy(src, r, xs_ref, p0_ref[base + r], row_sem).start(priority=r % 2)
        _row_copy(src, r, xs_ref, p1_ref[base + r], row_sem).start(priority=(r + 1) % 2)

    @pl.when(step == n_steps - 1)
    def _():
        drain_rows()


def _dispatch(pos0, pos1, h2):
    tm = TM_DISPATCH
    grid_spec = pltpu.PrefetchScalarGridSpec(
        num_scalar_prefetch=2,
        grid=(TOKENS // tm,),
        in_specs=[pl.BlockSpec(memory_space=pl.ANY),
                  pl.BlockSpec(memory_space=pl.ANY)],
        out_specs=pl.BlockSpec(memory_space=pl.ANY),
        scratch_shapes=[pltpu.VMEM((2, tm, D_MODEL), F32), pltpu.SemaphoreType.DMA((2,)),
                        pltpu.SemaphoreType.DMA(())],
    )
    return pl.pallas_call(
        _dispatch_kernel,
        grid_spec=grid_spec,
        out_shape=jax.ShapeDtypeStruct((MOE_ROWS, D_MODEL), F32),
        input_output_aliases={3: 0},
        compiler_params=_params(("arbitrary",), 32),
        name="moe_dispatch",
    )(pos0, pos1, h2, jnp.zeros((MOE_ROWS, D_MODEL), F32))


def _moe_ffn_kernel(te_ref, valid_ref, x_ref, wg_ref, wu_ref, wd_ref, o_ref, acc_ref):
    i = pl.program_id(0)
    j = pl.program_id(1)
    last = j == pl.num_programs(1) - 1

    @pl.when(valid_ref[i] == 1)
    def _():
        _swiglu_accumulate(x_ref[...].astype(BF16), wg_ref[0, 0], wu_ref[0, 0], wd_ref[0, 0], acc_ref, j)

        @pl.when(last)
        def _():
            o_ref[...] = acc_ref[...]

    @pl.when(jnp.logical_and(valid_ref[i] == 0, last))
    def _():
        o_ref[...] = jnp.zeros_like(o_ref)


def _moe_ffn(m, tile_expert, tile_valid, xs, wg, wu, wd):
    tm = TM_FFN
    row = pl.BlockSpec((tm, D_MODEL), lambda i, j, te, va: (i, 0))
    grid_spec = pltpu.PrefetchScalarGridSpec(
        num_scalar_prefetch=2,
        grid=(N_MOE_TILES, D_FF // TF),
        in_specs=[row,
                  pl.BlockSpec((1, 1, D_MODEL, TF), lambda i, j, te, va: (m, te[i], 0, j * va[i])),
                  pl.BlockSpec((1, 1, D_MODEL, TF), lambda i, j, te, va: (m, te[i], 0, j * va[i])),
                  pl.BlockSpec((1, 1, TF, D_MODEL), lambda i, j, te, va: (m, te[i], j * va[i], 0))],
        out_specs=row,
        scratch_shapes=[pltpu.VMEM((tm, D_MODEL), F32)],
    )
    return pl.pallas_call(
        _moe_ffn_kernel,
        grid_spec=grid_spec,
        out_shape=jax.ShapeDtypeStruct((MOE_ROWS, D_MODEL), F32),
        compiler_params=_params(("arbitrary", "arbitrary"), 56),
        name="ffn_moe",
    )(tile_expert, tile_valid, xs, wg, wu, wd)


def _moe_combine_kernel(p0_ref, p1_ref, x_ref, g2_ref, wt_ref, y_ref, o_ref, buf_ref, sem):
    tm = TM_COMBINE
    step = pl.program_id(0)
    slot = step % 2

    def gather(tile, sl):
        base = tile * tm

        for r in range(tm):
            _row_copy(y_ref, p0_ref[base + r], buf_ref.at[sl, 0], r, sem.at[sl]).start(priority=r % 2)
            _row_copy(y_ref, p1_ref[base + r], buf_ref.at[sl, 1], r, sem.at[sl]).start(priority=(r + 1) % 2)

    @pl.when(step == 0)
    def _():
        gather(step, slot)

    @pl.when(step + 1 < pl.num_programs(0))
    def _():
        gather(step + 1, 1 - slot)

    for k in range(2):
        pltpu.make_async_copy(y_ref.at[pl.ds(0, tm)], buf_ref.at[slot, k], sem.at[slot]).wait()
    w = wt_ref[...]
    f = w[:, 0:1] * buf_ref[slot, 0] + w[:, 1:2] * buf_ref[slot, 1]
    o_ref[...] = x_ref[...] + g2_ref[0] * f


def _moe_combine(pos0, pos1, x, gate2, wts, y):
    tm = TM_COMBINE
    per_batch = SEQ // tm
    row = pl.BlockSpec((tm, D_MODEL), lambda i, p0, p1: (i, 0))
    grid_spec = pltpu.PrefetchScalarGridSpec(
        num_scalar_prefetch=2,
        grid=(TOKENS // tm,),
        in_specs=[row,
                  pl.BlockSpec((1, 1, D_MODEL), lambda i, p0, p1: (i // per_batch, 0, 0)),
                  pl.BlockSpec((tm, LANES), lambda i, p0, p1: (i, 0)),
                  pl.BlockSpec(memory_space=pl.ANY)],
        out_specs=row,
        scratch_shapes=[pltpu.VMEM((2, 2, tm, D_MODEL), F32), pltpu.SemaphoreType.DMA((2,))],
    )
    return pl.pallas_call(
        _moe_combine_kernel,
        grid_spec=grid_spec,
        out_shape=jax.ShapeDtypeStruct((TOKENS, D_MODEL), F32),
        compiler_params=_params(("arbitrary",), 32),
        name="moe_combine",
    )(pos0, pos1, x, gate2, wts, y)


def _prep_w_in(w_in):
    cq_ckv = w_in[..., 0:Q_RANK + KV_RANK]
    kr = w_in[..., Q_RANK + KV_RANK:Q_RANK + KV_RANK + ROPE]
    rest = w_in[..., Q_RANK + KV_RANK + ROPE:]
    x1, x2 = kr[..., :ROPE // 2], kr[..., ROPE // 2:]
    z = lambda n: jnp.zeros(kr.shape[:-1] + (n,), w_in.dtype)
    kr_a = jnp.concatenate([z(NOPE), x1, x2, z(HEAD_TILE - QK_DIM)], axis=-1)
    kr_b = jnp.concatenate([z(NOPE), -x2, x1, z(HEAD_TILE - QK_DIM)], axis=-1)
    return jnp.concatenate([cq_ckv, kr_a, kr_b, rest], axis=-1).astype(BF16)


def _prep_w_uq(w_uq):
    w = w_uq.reshape(DEPTH, Q_RANK, N_HEADS, QK_DIM)
    nope, x1, x2 = w[..., :NOPE], w[..., NOPE:NOPE + ROPE // 2], w[..., NOPE + ROPE // 2:]
    z = lambda n: jnp.zeros(w.shape[:-1] + (n,), w.dtype)
    qa = jnp.concatenate([nope, x1, x2, z(HEAD_TILE - QK_DIM)], axis=-1)
    qb = jnp.concatenate([z(NOPE), -x2, x1, z(HEAD_TILE - QK_DIM)], axis=-1)
    flat = lambda a: a.reshape(DEPTH, Q_RANK, N_HEADS * HEAD_TILE)
    return jnp.concatenate([flat(qa), flat(qb)], axis=-1).astype(BF16)


def _prep_w_ukv(w_ukv):
    w = w_ukv.reshape(DEPTH, KV_RANK, N_HEADS, NOPE + V_DIM)
    k_nope, v = w[..., :NOPE], w[..., NOPE:]
    zk = jnp.zeros(k_nope.shape[:-1] + (HEAD_TILE - NOPE,), w.dtype)
    ka = jnp.concatenate([k_nope, zk], axis=-1)
    zv = jnp.zeros_like(v)
    even = (jnp.arange(N_HEADS) % 2 == 0)[None, None, :, None]
    vp = jnp.concatenate([jnp.where(even, v, zv), jnp.where(even, zv, v)], axis=-1)
    flat = lambda a: a.reshape(DEPTH, KV_RANK, N_HEADS * HEAD_TILE)
    return jnp.concatenate([flat(ka), flat(vp)], axis=-1).astype(BF16)


def _head_gain(g, scale):
    pad = jnp.zeros((DEPTH, HEAD_TILE - QK_DIM), g.dtype)
    return (jnp.concatenate([g, pad], axis=-1) * scale).reshape(DEPTH, 1, HEAD_TILE)


def _rope_tables(positions):
    inv_freq = 1.0 / (10000.0 ** (jnp.arange(0, ROPE, 2, dtype=F32) / ROPE))
    ang = positions.astype(F32).reshape(TOKENS, 1) * inv_freq
    cos, sin = jnp.cos(ang), jnp.sin(ang)
    cos_t = jnp.concatenate([jnp.ones((TOKENS, NOPE), F32), cos, cos,
                             jnp.zeros((TOKENS, HEAD_TILE - QK_DIM), F32)], axis=-1)
    sin_t = jnp.concatenate([jnp.zeros((TOKENS, NOPE), F32), sin, sin,
                             jnp.zeros((TOKENS, HEAD_TILE - QK_DIM), F32)], axis=-1)
    return cos_t, sin_t


def _moe_plan(meta, cnt):
    e1, e2, r1, r2 = meta[:, 0], meta[:, 1], meta[:, 2], meta[:, 3]
    counts = cnt[0, :N_EXPERTS].astype(jnp.int32)
    padded = ((counts + TM_FFN - 1) // TM_FFN) * TM_FFN
    ends = jnp.cumsum(padded)
    starts = ends - padded
    experts = jnp.arange(N_EXPERTS, dtype=jnp.int32)
    start_of = lambda e: jnp.sum(jnp.where(e[:, None] == experts[None, :], starts[None, :], 0), axis=1)
    pos0 = start_of(e1) + r1
    pos1 = start_of(e2) + r2
    tile_start = jnp.arange(N_MOE_TILES, dtype=jnp.int32) * TM_FFN
    tile_expert = jnp.minimum(jnp.sum(tile_start[:, None] >= ends[None, :], axis=1), N_EXPERTS - 1)
    tile_valid = (tile_start < ends[-1]).astype(jnp.int32)
    return pos0.astype(jnp.int32), pos1.astype(jnp.int32), tile_expert.astype(jnp.int32), tile_valid


def kernel(x, c, positions, ada_w, ada_b, norm1_g, norm2_g, w_in, q_norm_g, kv_norm_g, w_uq, w_ukv, q_head_g,
           k_head_g, w_o_mla, conv_w, conv_b, lru_gate_w, lru_gate_b, lru_a_param, w_o_lru, w_out, ffn_w_gate,
           ffn_w_up, ffn_w_down, moe_router, moe_w_gate, moe_w_up, moe_w_down):
    cos_t, sin_t = _rope_tables(positions)
    mod = _modulation(c, ada_w, ada_b).reshape(DEPTH, BATCH, 6, 1, D_MODEL)

    vec = lambda a: a.reshape(DEPTH, 1, a.shape[-1])
    w_in_p = _prep_w_in(w_in)
    wq_p = _prep_w_uq(w_uq)
    wkv_p = _prep_w_ukv(w_ukv)
    hq = _head_gain(q_head_g, 0.5 * TWO_LOG2E * QK_DIM ** -0.5)
    hk = _head_gain(k_head_g, 1.0)
    wg_p = (0.5 * lru_gate_w).transpose(0, 3, 4, 1, 2, 5).reshape(
        DEPTH, N_LRU_BLOCKS, LRU_BLOCK, 4 * LRU_BLOCK).astype(BF16)
    bg_p = (0.5 * lru_gate_b).reshape(DEPTH, 2, 2, N_LRU_BLOCKS, LRU_BLOCK).transpose(0, 3, 1, 2, 4).reshape(
        DEPTH, N_LRU_BLOCKS, 1, 4 * LRU_BLOCK)
    ap_p = lru_a_param.reshape(DEPTH, 2, N_LRU_BLOCKS, LRU_BLOCK).transpose(0, 2, 1, 3).reshape(
        DEPTH, N_LRU_BLOCKS, 1, 2 * LRU_BLOCK)
    conv_b_p = conv_b.reshape(DEPTH, 1, LRU_W)
    wol, wom, wout = w_o_lru.astype(BF16), w_o_mla.astype(BF16), w_out.astype(BF16)
    fg, fu, fd = ffn_w_gate.astype(BF16), ffn_w_up.astype(BF16), ffn_w_down.astype(BF16)
    mg, mu, md = moe_w_gate.astype(BF16), moe_w_up.astype(BF16), moe_w_down.astype(BF16)
    router_hi = moe_router.astype(BF16)
    router_lo = (moe_router - router_hi.astype(F32)).astype(BF16)
    router = jnp.pad(jnp.concatenate([router_hi, router_lo], axis=-1),
                     ((0, 0), (0, 0), (0, LANES - 2 * N_EXPERTS)))
    n1, n2, gq, gkv = vec(norm1_g), vec(norm2_g), vec(q_norm_g), vec(kv_norm_g)

    xt = x.reshape(TOKENS, D_MODEL)
    for l in range(DEPTH):
        shift1, scale1, gate1, shift2, scale2, gate2 = [mod[l, :, k] for k in range(6)]
        q, k, v, u, y_gate, gb = _in_projection(l, xt, shift1, scale1, n1, w_in_p, cos_t, sin_t, gq, gkv,
                                                wq_p, wkv_p, hq, hk)
        attn = _attention(q, k, v).reshape(TOKENS, N_HEADS * V_DIM)
        h_lru = _lru_branch(l, u.reshape(BATCH, SEQ, LRU_W), conv_w, conv_b_p, wg_p, bg_p, ap_p)
        h_lru = h_lru.reshape(TOKENS, LRU_W)
        m = l // 2
        if l % 2 == 0:
            xt, h2 = _mix(l, h_lru, y_gate, attn, gb, xt, gate1, shift2, scale2, n2, wol, wom, wout)
            xt = _dense_ffn(m, h2, xt, gate2, fg, fu, fd)
        else:
            xt, h2, meta, wts, cnt = _mix(l, h_lru, y_gate, attn, gb, xt, gate1, shift2, scale2, n2, wol, wom,
                                          wout, router=router)
            pos0, pos1, tile_expert, tile_valid = _moe_plan(meta, cnt)
            xs = _dispatch(pos0, pos1, h2)
            y = _moe_ffn(m, tile_expert, tile_valid, xs, mg, mu, md)
            xt = _moe_combine(pos0, pos1, xt, gate2, wts, y)
    return xt.reshape(BATCH, SEQ, D_MODEL)
```

```python
import functools

import jax
import jax.numpy as jnp
from jax import lax
from jax.experimental import pallas as pl
from jax.experimental.pallas import tpu as pltpu

F32 = jnp.float32
BF16 = jnp.bfloat16

D_MODEL = 1024
BATCH = 8
SEQ = 2048
TOKENS = BATCH * SEQ
DEPTH = 4
N_HEADS = 8
NOPE = 64
ROPE = 32
QK_DIM = NOPE + ROPE
V_DIM = 64
Q_RANK = 256
KV_RANK = 256
LRU_W = D_MODEL
N_LRU_BLOCKS = 8
LRU_BLOCK = LRU_W // N_LRU_BLOCKS
RG_LRU_C = 8.0
D_FF = 2816
N_EXPERTS = 8
N_MOE_LAYERS = DEPTH // 2
EPS = 1e-6

LANES = 128
SUBLANES = 8
HEAD_TILE = LANES

MLA_COLS = Q_RANK + KV_RANK + 2 * HEAD_TILE
IN_COLS = MLA_COLS + 2 * LRU_W + 2 * D_MODEL

TM_PROJ = 512
TQ = 512
TM_FFN = 512
TF = D_FF // 2
N_MOE_TILES = (2 * TOKENS) // TM_FFN + N_EXPERTS
MOE_ROWS = N_MOE_TILES * TM_FFN
TM_DISPATCH = 512
TM_COMBINE = 256

MIB = 1024 * 1024


def _params(sem, vmem_mib):
    return pltpu.CompilerParams(dimension_semantics=sem, vmem_limit_bytes=vmem_mib * MIB)


def _dot(a, b):
    return jnp.dot(a, b, preferred_element_type=F32)


def _split_bf16(a):
    hi = a.astype(BF16)
    lo = (a - hi.astype(F32)).astype(BF16)
    return hi, lo


def _rms(v, inv_n):
    return v * lax.rsqrt(jnp.sum(v * v, axis=-1, keepdims=True) * inv_n + EPS)


def _layer_spec(a, l):
    nd = a.ndim - 1
    return pl.BlockSpec((1,) + a.shape[1:], lambda i: (l,) + (0,) * nd)


TN_MOD = 1536


def _mod_kernel(c_ref, w_ref, b_ref, o_ref):
    c = c_ref[...]
    ch, cl = _split_bf16(c * jax.nn.sigmoid(c))
    wh, wl = _split_bf16(w_ref[0])
    o_ref[0] = _dot(ch, wh) + (_dot(ch, wl) + _dot(cl, wh)) + b_ref[0]


def _modulation(c, ada_w, ada_b):
    n = 6 * D_MODEL
    return pl.pallas_call(
        _mod_kernel,
        grid=(DEPTH, n // TN_MOD),
        in_specs=[
            pl.BlockSpec((BATCH, D_MODEL), lambda l, j: (0, 0)),
            pl.BlockSpec((1, D_MODEL, TN_MOD), lambda l, j: (l, 0, j)),
            pl.BlockSpec((1, 1, TN_MOD), lambda l, j: (l, 0, j)),
        ],
        out_specs=pl.BlockSpec((1, BATCH, TN_MOD), lambda l, j: (l, 0, j)),
        out_shape=jax.ShapeDtypeStruct((DEPTH, BATCH, n), F32),
        compiler_params=_params(("arbitrary", "arbitrary"), 40),
        name="adaln_mod",
    )(c, ada_w, ada_b.reshape(DEPTH, 1, n))


def _batch_vec_spec(tm):
    per_batch = SEQ // tm
    return pl.BlockSpec((1, 1, D_MODEL), lambda i: (i // per_batch, 0, 0))


def _resident_spec(a, l):
    nd = a.ndim - 1
    return pl.BlockSpec((1,) + a.shape[1:], lambda i: (l,) + (0,) * nd, pipeline_mode=pl.Buffered(1))


def _inproj_kernel(x_ref, sh_ref, sc_ref, g_ref, w_ref, cos_ref, sin_ref, gq_ref, gkv_ref, wq_ref, wkv_ref,
                   hq_ref, hk_ref, q_ref, k_ref, v_ref, u_ref, y_ref, gb_ref):
    x = x_ref[...]
    h = _rms(x, 1.0 / D_MODEL) * g_ref[0]
    h = (h * (1.0 + sc_ref[0]) + sh_ref[0]).astype(BF16)
    c0, c1, c2 = MLA_COLS, MLA_COLS + LRU_W, MLA_COLS + 2 * LRU_W
    m = _dot(h, w_ref[0, :, 0:c0])
    u_ref[...] = _dot(h, w_ref[0, :, c0:c1])
    y_ref[...] = _dot(h, w_ref[0, :, c1:c2]).astype(BF16)
    gb_ref[...] = _dot(h, w_ref[0, :, c2:IN_COLS]).astype(BF16)

    cos = cos_ref[...]
    sin = sin_ref[...]
    cq = m[:, 0:Q_RANK]
    ckv = m[:, Q_RANK:Q_RANK + KV_RANK]
    kr_a = m[:, Q_RANK + KV_RANK:Q_RANK + KV_RANK + HEAD_TILE]
    kr_b = m[:, Q_RANK + KV_RANK + HEAD_TILE:MLA_COLS]
    cqn = (_rms(cq, 1.0 / Q_RANK) * gq_ref[0]).astype(BF16)
    ckvn = (_rms(ckv, 1.0 / KV_RANK) * gkv_ref[0]).astype(BF16)
    qq = _dot(cqn, wq_ref[0])
    kv = _dot(ckvn, wkv_ref[0])
    k_pe = kr_a * cos + kr_b * sin
    nq = N_HEADS * HEAD_TILE
    lane = lax.broadcasted_iota(jnp.int32, (TM_PROJ, HEAD_TILE), 1)
    for hd in range(N_HEADS):
        lo, hi = hd * HEAD_TILE, (hd + 1) * HEAD_TILE
        qh = qq[:, lo:hi] * cos + qq[:, nq + lo:nq + hi] * sin
        q_ref[0, hd] = (_rms(qh, 1.0 / QK_DIM) * hq_ref[0]).astype(BF16)
        kh = kv[:, lo:hi] + k_pe
        k_ref[0, hd] = (_rms(kh, 1.0 / QK_DIM) * hk_ref[0]).astype(BF16)
        v_ref[0, hd] = jnp.where(lane == V_DIM * (1 - hd % 2), 1.0, kv[:, nq + lo:nq + hi]).astype(BF16)


def _in_projection(l, x, shift, scale, gain, w, cos_t, sin_t, gq, gkv, wq, wkv, hq, hk):
    tm = TM_PROJ
    per_batch = SEQ // tm
    row = lambda n: pl.BlockSpec((tm, n), lambda i: (i, 0))
    head = pl.BlockSpec((1, N_HEADS, tm, HEAD_TILE), lambda i: (i // per_batch, 0, i % per_batch, 0))
    head_shape = jax.ShapeDtypeStruct((BATCH, N_HEADS, SEQ, HEAD_TILE), BF16)
    params = [gq, gkv, wq, wkv, hq, hk]
    return pl.pallas_call(
        _inproj_kernel,
        grid=(TOKENS // tm,),
        in_specs=[row(D_MODEL), _batch_vec_spec(tm), _batch_vec_spec(tm), _layer_spec(gain, l),
                  _resident_spec(w, l), row(HEAD_TILE), row(HEAD_TILE)] + [_resident_spec(p, l) for p in params],
        out_specs=[head, head, head, row(LRU_W), row(LRU_W), row(2 * D_MODEL)],
        out_shape=[
            head_shape, head_shape, head_shape,
            jax.ShapeDtypeStruct((TOKENS, LRU_W), F32),
            jax.ShapeDtypeStruct((TOKENS, LRU_W), BF16),
            jax.ShapeDtypeStruct((TOKENS, 2 * D_MODEL), BF16),
        ],
        compiler_params=_params(("arbitrary",), 56),
        name="in_proj",
    )(x, shift, scale, gain, w, cos_t, sin_t, *params)


def _attn_kernel(q_ref, k_ref, v_ref, o_ref):
    lane = lax.broadcasted_iota(jnp.int32, (TQ, LANES), 1)
    for pair in range(N_HEADS // 2):
        acc = None
        for j in range(2):
            h = 2 * pair + j
            s = lax.dot_general(q_ref[0, h], k_ref[0, h], (((1,), (1,)), ((), ())),
                                preferred_element_type=F32)
            p = jnp.exp2(s - jnp.max(s, axis=-1, keepdims=True)).astype(BF16)
            o = _dot(p, v_ref[0, h])
            ones_lane = V_DIM * (1 - j)
            inv = 1.0 / o[:, ones_lane:ones_lane + 1]
            mine = (lane < V_DIM) if j == 0 else (lane >= V_DIM)
            o = jnp.where(mine, o * inv, 0.0)
            acc = o if acc is None else acc + o
        o_ref[0, :, pair * LANES:(pair + 1) * LANES] = acc.astype(BF16)


def _attention(q, k, v):
    nq = SEQ // TQ
    kv_spec = pl.BlockSpec((1, N_HEADS, SEQ, HEAD_TILE), lambda b, i: (b, 0, 0, 0))
    return pl.pallas_call(
        _attn_kernel,
        grid=(BATCH, nq),
        in_specs=[pl.BlockSpec((1, N_HEADS, TQ, HEAD_TILE), lambda b, i: (b, 0, i, 0)), kv_spec, kv_spec],
        out_specs=pl.BlockSpec((1, TQ, N_HEADS * V_DIM), lambda b, i: (b, i, 0)),
        out_shape=jax.ShapeDtypeStruct((BATCH, SEQ, N_HEADS * V_DIM), BF16),
        compiler_params=_params(("arbitrary", "arbitrary"), 48),
        name="attention",
    )(q, k, v)


CHUNK = SEQ // SUBLANES
PITCH = CHUNK + 4
CONV_BEFORE, CONV_AFTER = 2, 1
TWO_LOG2E = 2.0 * 1.4426950408889634


def _shift_rows(v, row, down):
    if down:
        return jnp.where(row == 0, 0.0, pltpu.roll(v, 1, 0))
    return jnp.where(row == SUBLANES - 1, 0.0, pltpu.roll(v, SUBLANES - 1, 0))


def _scan_sublanes(a, b, row, reverse):
    for s in (1, 2, 4):
        shift = SUBLANES - s if reverse else s
        a_s = pltpu.roll(a, shift, 0)
        b_s = pltpu.roll(b, shift, 0)
        keep = (row < SUBLANES - s) if reverse else (row >= s)
        b = jnp.where(keep, a * b_s + b, b)
        a = jnp.where(keep, a * a_s, a)
    return b


def _lru_kernel(u_hbm, cw_ref, cb_ref, wg_ref, bg_ref, ap_ref, h_hbm,
                uin_ref, hout_ref, up_ref, uc_ref, g_ref, a_ref, b_ref, h_ref, p_ref, in_sem, out_sem):
    n_steps = pl.num_programs(0) * N_LRU_BLOCKS
    step = pl.program_id(0) * N_LRU_BLOCKS + pl.program_id(1)
    slot = step % 2

    def chunk_copies(s, sl, inbound):
        bb = s // N_LRU_BLOCKS
        col = pl.multiple_of((s % N_LRU_BLOCKS) * LRU_BLOCK, LRU_BLOCK)
        out = []
        for j in range(SUBLANES):
            hbm = (u_hbm if inbound else h_hbm).at[bb, pl.ds(j * CHUNK, CHUNK), pl.ds(col, LRU_BLOCK)]
            if inbound:
                out.append(pltpu.make_async_copy(hbm, uin_ref.at[sl, pl.ds(j * PITCH, CHUNK), :], in_sem.at[sl]))
            else:
                out.append(pltpu.make_async_copy(hout_ref.at[sl, pl.ds(j * PITCH, CHUNK), :], hbm, out_sem.at[sl]))
        return out

    @pl.when(step == 0)
    def _():
        for cp in chunk_copies(step, slot, True):
            cp.start()

    @pl.when(step + 1 < n_steps)
    def _():
        for cp in chunk_copies(step + 1, 1 - slot, True):
            cp.start()

    for cp in chunk_copies(step, slot, True):
        cp.wait()

    row = lax.broadcasted_iota(jnp.int32, (SUBLANES, LRU_BLOCK), 0)

    def tile_rows(k):
        return slice(k * SUBLANES, (k + 1) * SUBLANES)

    for k in range(CHUNK):
        up_ref[tile_rows(k + CONV_BEFORE), :] = uin_ref[slot, pl.ds(k, SUBLANES, stride=PITCH), :]
    for i in range(CONV_BEFORE):
        src = up_ref[(CHUNK + i) * SUBLANES:(CHUNK + i + 1) * SUBLANES, :]
        up_ref[i * SUBLANES:(i + 1) * SUBLANES, :] = _shift_rows(src, row, True)
    for i in range(CONV_AFTER):
        src = up_ref[(CONV_BEFORE + i) * SUBLANES:(CONV_BEFORE + i + 1) * SUBLANES, :]
        dst = (CONV_BEFORE + CHUNK + i) * SUBLANES
        up_ref[dst:dst + SUBLANES, :] = _shift_rows(src, row, False)

    uc = cb_ref[0]
    for tap in range(CONV_BEFORE + CONV_AFTER + 1):
        uc = uc + up_ref[tap * SUBLANES:tap * SUBLANES + SEQ, :] * cw_ref[0, tap:tap + 1, :]
    uc_ref[...] = uc
    g_ref[...] = _dot(uc.astype(BF16), wg_ref[0, 0]) + bg_ref[0, 0]

    sp = jax.nn.softplus(ap_ref[0, 0])
    u_half = 0.5 * uc_ref[...]
    for d in range(2):
        quarter_coef = (-0.25 * RG_LRU_C) * sp[:, d * LRU_BLOCK:(d + 1) * LRU_BLOCK]
        g_r = g_ref[:, 2 * d * LRU_BLOCK:(2 * d + 1) * LRU_BLOCK]
        g_i = g_ref[:, (2 * d + 1) * LRU_BLOCK:(2 * d + 2) * LRU_BLOCK]
        x = quarter_coef + quarter_coef * jnp.tanh(g_r)
        a = jnp.exp2(TWO_LOG2E * x)
        s = -jnp.tanh(x)
        mult = (1.0 + a) * jnp.where(s > 0.0, s * lax.rsqrt(s), 0.0)
        gated = u_half + u_half * jnp.tanh(g_i)
        a_ref[d] = a
        b_ref[d] = mult * gated
        first = slice(0, SUBLANES) if d == 0 else slice(SEQ - SUBLANES, SEQ)
        first_row = 0 if d == 0 else SUBLANES - 1
        g_first = u_half[first, :] + u_half[first, :] * jnp.tanh(g_i[first, :])
        b_ref[d, first, :] = jnp.where(row == first_row, g_first, b_ref[d, first, :])

    zeros = jnp.zeros((SUBLANES, LRU_BLOCK), F32)
    ones = jnp.ones((SUBLANES, LRU_BLOCK), F32)
    state = [(zeros, ones), (zeros, ones)]
    for m in range(CHUNK // 2):
        for d in range(2):
            k0, k1 = (2 * m, 2 * m + 1) if d == 0 else (CHUNK - 1 - 2 * m, CHUNK - 2 - 2 * m)
            r0, r1 = tile_rows(k0), tile_rows(k1)
            a0, b0 = a_ref[d, r0, :], b_ref[d, r0, :]
            a1, b1 = a_ref[d, r1, :], b_ref[d, r1, :]
            a01 = a1 * a0
            b01 = a1 * b0 + b1
            h, p = state[d]
            h_ref[d, r0, :] = a0 * h + b0
            p_ref[d, r0, :] = a0 * p
            h = a01 * h + b01
            p = a01 * p
            h_ref[d, r1, :] = h
            p_ref[d, r1, :] = p
            state[d] = (h, p)
    (hf, af), (hb, ab) = state
    carry_f = _shift_rows(_scan_sublanes(af, hf, row, False), row, True)
    carry_b = _shift_rows(_scan_sublanes(ab, hb, row, True), row, False)

    @pl.when(step >= 2)
    def _():
        for cp in chunk_copies(step, slot, False):
            cp.wait()

    for k in range(CHUNK):
        r = tile_rows(k)
        h = (h_ref[0, r, :] + p_ref[0, r, :] * carry_f) + (h_ref[1, r, :] + p_ref[1, r, :] * carry_b)
        hout_ref[slot, pl.ds(k, SUBLANES, stride=PITCH), :] = h
    for cp in chunk_copies(step, slot, False):
        cp.start()

    @pl.when(step == n_steps - 1)
    def _():
        for cp in chunk_copies(step, 1 - slot, False) + chunk_copies(step, slot, False):
            cp.wait()


def _lru_branch(l, u, conv_w, conv_b, wg, bg, a_param):
    batch = u.shape[0]
    vm = lambda *shape: pltpu.VMEM(shape, F32)
    stage = (2, SUBLANES * PITCH, LRU_BLOCK)
    return pl.pallas_call(
        _lru_kernel,
        grid=(batch, N_LRU_BLOCKS),
        in_specs=[
            pl.BlockSpec(memory_space=pl.ANY),
            pl.BlockSpec((1, 4, LRU_BLOCK), lambda b, n: (l, 0, n)),
            pl.BlockSpec((1, 1, LRU_BLOCK), lambda b, n: (l, 0, n)),
            pl.BlockSpec((1, 1, LRU_BLOCK, 4 * LRU_BLOCK), lambda b, n: (l, n, 0, 0)),
            pl.BlockSpec((1, 1, 1, 4 * LRU_BLOCK), lambda b, n: (l, n, 0, 0)),
            pl.BlockSpec((1, 1, 1, 2 * LRU_BLOCK), lambda b, n: (l, n, 0, 0)),
        ],
        out_specs=pl.BlockSpec(memory_space=pl.ANY),
        out_shape=jax.ShapeDtypeStruct((batch, SEQ, LRU_W), F32),
        scratch_shapes=[vm(*stage), vm(*stage),
                        vm(SEQ + (CONV_BEFORE + CONV_AFTER) * SUBLANES, LRU_BLOCK), vm(SEQ, LRU_BLOCK),
                        vm(SEQ, 4 * LRU_BLOCK)] + [vm(2, SEQ, LRU_BLOCK)] * 4 + [
                        pltpu.SemaphoreType.DMA((2,)), pltpu.SemaphoreType.DMA((2,))],
        compiler_params=_params(("arbitrary", "arbitrary"), 40),
        name="rg_lru",
    )(u, conv_w, conv_b, wg, bg, a_param)


def _gelu_tanh(x):
    return 0.5 * x * (1.0 + jnp.tanh(0.7978845608028654 * (x + 0.044715 * x * x * x)))


def _route(lg, meta_ref, wt_ref, cnt_ref, carry_ref):
    tm = lg.shape[0]

    @pl.when(pl.program_id(0) == 0)
    def _():
        carry_ref[...] = jnp.zeros_like(carry_ref)

    lane = lax.broadcasted_iota(jnp.int32, (tm, LANES), 1)
    lane_f = lane.astype(F32)
    neg = jnp.float32(-jnp.inf)
    lg = jnp.where(lane < N_EXPERTS, lg, neg)
    m1 = jnp.max(lg, axis=-1, keepdims=True)
    i1 = jnp.min(jnp.where(lg == m1, lane_f, float(LANES)), axis=-1, keepdims=True).astype(jnp.int32)
    lg2 = jnp.where(lane == i1, neg, lg)
    m2 = jnp.max(lg2, axis=-1, keepdims=True)
    i2 = jnp.min(jnp.where(lg2 == m2, lane_f, float(LANES)), axis=-1, keepdims=True).astype(jnp.int32)
    e = jnp.exp(m2 - m1)
    w1 = 1.0 / (1.0 + e)
    w2 = e * w1
    sel = jnp.where(lane == i1, 1.0, jnp.where(lane == i2, 1.0, 0.0))
    r_i = lax.broadcasted_iota(jnp.int32, (tm, tm), 0)
    c_i = lax.broadcasted_iota(jnp.int32, (tm, tm), 1)
    tri = jnp.where(r_i > c_i, 1.0, 0.0).astype(BF16)
    cum = _dot(tri, sel.astype(BF16)) + carry_ref[0:1, :]
    r1 = jnp.sum(jnp.where(lane == i1, cum, 0.0), axis=-1, keepdims=True).astype(jnp.int32)
    r2 = jnp.sum(jnp.where(lane == i2, cum, 0.0), axis=-1, keepdims=True).astype(jnp.int32)
    total = carry_ref[0:1, :] + jnp.sum(sel, axis=0, keepdims=True)
    carry_ref[...] = jnp.broadcast_to(total, carry_ref.shape)
    cnt_ref[...] = jnp.broadcast_to(total, cnt_ref.shape)
    meta_ref[...] = jnp.where(lane == 0, i1, jnp.where(lane == 1, i2, jnp.where(lane == 2, r1,
                              jnp.where(lane == 3, r2, 0))))
    wt_ref[...] = jnp.where(lane == 0, w1, jnp.where(lane == 1, w2, 0.0))


def _mix_kernel(*refs, moe):
    if moe:
        (h_ref, y_ref, at_ref, gb_ref, x_ref, g1_ref, sh_ref, sc_ref, n2_ref, wol_ref, wom_ref, wout_ref,
         wr_ref, xo_ref, h2_ref, meta_ref, wt_ref, cnt_ref, carry_ref) = refs
    else:
        (h_ref, y_ref, at_ref, gb_ref, x_ref, g1_ref, sh_ref, sc_ref, n2_ref, wol_ref, wom_ref, wout_ref,
         xo_ref, h2_ref) = refs
    hg = (h_ref[...] * _gelu_tanh(y_ref[...].astype(F32))).astype(BF16)
    y_lru = _dot(hg, wol_ref[0])
    y_mla = _dot(at_ref[...], wom_ref[0])
    g_lru = jax.nn.sigmoid(gb_ref[:, 0:D_MODEL].astype(F32))
    g_mla = jax.nn.sigmoid(gb_ref[:, D_MODEL:2 * D_MODEL].astype(F32))
    z = (g_lru * y_lru + g_mla * y_mla).astype(BF16)
    xn = x_ref[...] + g1_ref[0] * _dot(z, wout_ref[0])
    xo_ref[...] = xn
    h2 = _rms(xn, 1.0 / D_MODEL) * n2_ref[0]
    h2 = h2 * (1.0 + sc_ref[0]) + sh_ref[0]
    if moe:
        h2_ref[...] = h2
        hh, hl = _split_bf16(h2)
        parts = _dot(hh, wr_ref[0]) + _dot(hl, wr_ref[0])
        _route(parts + pltpu.roll(parts, LANES - N_EXPERTS, 1), meta_ref, wt_ref, cnt_ref, carry_ref)
    else:
        h2_ref[...] = h2.astype(BF16)


def _mix(l, h_lru, y_gate, attn, gb, x, gate1, shift2, scale2, n2, wol, wom, wout, router=None):
    tm = TM_PROJ
    moe = router is not None
    row = lambda n: pl.BlockSpec((tm, n), lambda i: (i, 0))
    vec = _batch_vec_spec(tm)
    ins = [h_lru, y_gate, attn, gb, x, gate1, shift2, scale2, n2, wol, wom, wout]
    in_specs = [row(LRU_W), row(LRU_W), row(N_HEADS * V_DIM), row(2 * D_MODEL), row(D_MODEL), vec, vec, vec,
                _layer_spec(n2, l), _layer_spec(wol, l), _layer_spec(wom, l), _layer_spec(wout, l)]
    out_specs = [row(D_MODEL), row(D_MODEL)]
    out_shape = [jax.ShapeDtypeStruct((TOKENS, D_MODEL), F32),
                 jax.ShapeDtypeStruct((TOKENS, D_MODEL), F32 if moe else BF16)]
    scratch = []
    if moe:
        ins.append(router)
        in_specs.append(_layer_spec(router, l // 2))
        out_specs += [row(LANES), row(LANES), pl.BlockSpec((SUBLANES, LANES), lambda i: (0, 0))]
        out_shape += [jax.ShapeDtypeStruct((TOKENS, LANES), jnp.int32),
                      jax.ShapeDtypeStruct((TOKENS, LANES), F32),
                      jax.ShapeDtypeStruct((SUBLANES, LANES), F32)]
        scratch.append(pltpu.VMEM((SUBLANES, LANES), F32))
    return pl.pallas_call(
        functools.partial(_mix_kernel, moe=moe),
        grid=(TOKENS // tm,),
        in_specs=in_specs,
        out_specs=out_specs,
        out_shape=out_shape,
        scratch_shapes=scratch,
        compiler_params=_params(("arbitrary",), 48),
        name="mix_moe" if moe else "mix_dense",
    )(*ins)


def _swiglu_accumulate(h, wg, wu, wd, acc_ref, j):
    a = _dot(h, wg)
    b = _dot(h, wu)
    part = _dot((a * jax.nn.sigmoid(a) * b).astype(BF16), wd)

    @pl.when(j == 0)
    def _():
        acc_ref[...] = part

    @pl.when(j > 0)
    def _():
        acc_ref[...] += part


def _ffn_kernel(h_ref, x_ref, g2_ref, wg_ref, wu_ref, wd_ref, o_ref, acc_ref):
    j = pl.program_id(1)
    _swiglu_accumulate(h_ref[...], wg_ref[0], wu_ref[0], wd_ref[0], acc_ref, j)

    @pl.when(j == pl.num_programs(1) - 1)
    def _():
        o_ref[...] = x_ref[...] + g2_ref[0] * acc_ref[...]


def _dense_ffn(m, h2, x, gate2, wg, wu, wd):
    tm = TM_FFN
    per_batch = SEQ // tm
    row = pl.BlockSpec((tm, D_MODEL), lambda i, j: (i, 0))
    return pl.pallas_call(
        _ffn_kernel,
        grid=(TOKENS // tm, D_FF // TF),
        in_specs=[row, row, pl.BlockSpec((1, 1, D_MODEL), lambda i, j: (i // per_batch, 0, 0)),
                  pl.BlockSpec((1, D_MODEL, TF), lambda i, j: (m, 0, j)),
                  pl.BlockSpec((1, D_MODEL, TF), lambda i, j: (m, 0, j)),
                  pl.BlockSpec((1, TF, D_MODEL), lambda i, j: (m, j, 0))],
        out_specs=row,
        out_shape=jax.ShapeDtypeStruct((TOKENS, D_MODEL), F32),
        scratch_shapes=[pltpu.VMEM((tm, D_MODEL), F32)],
        compiler_params=_params(("arbitrary", "arbitrary"), 56),
        name="ffn_dense",
    )(h2, x, gate2, wg, wu, wd)


def _row_copy(src_ref, src_row, dst_ref, dst_row, sem):
    return pltpu.make_async_copy(src_ref.at[pl.ds(src_row, 1)], dst_ref.at[pl.ds(dst_row, 1)], sem)


def _dispatch_kernel(p0_ref, p1_ref, h_hbm, init_ref, xs_ref, buf_ref, load_sem, row_sem):
    del init_ref
    tm = TM_DISPATCH
    step = pl.program_id(0)
    n_steps = pl.num_programs(0)
    slot = step % 2
    base = step * tm

    def load(tile, sl):
        return pltpu.make_async_copy(h_hbm.at[pl.ds(tile * tm, tm)], buf_ref.at[sl], load_sem.at[sl])

    def drain_rows():
        for _ in range(2):
            pltpu.make_async_copy(buf_ref.at[0], xs_ref.at[pl.ds(0, tm)], row_sem).wait()

    @pl.when(step == 0)
    def _():
        load(step, slot).start()

    @pl.when(step > 0)
    def _():
        drain_rows()

    @pl.when(step + 1 < n_steps)
    def _():
        load(step + 1, 1 - slot).start()

    load(step, slot).wait()
    src = buf_ref.at[slot]

    for r in range(tm):
        _row_copy(src, r, xs_ref, p0_ref[base + r], row_sem).start(priority=r % 2)
        _row_copy(src, r, xs_ref, p1_ref[base + r], row_sem).start(priority=(r + 1) % 2)

    @pl.when(step == n_steps - 1)
    def _():
        drain_rows()


def _dispatch(pos0, pos1, h2):
    tm = TM_DISPATCH
    grid_spec = pltpu.PrefetchScalarGridSpec(
        num_scalar_prefetch=2,
        grid=(TOKENS // tm,),
        in_specs=[pl.BlockSpec(memory_space=pl.ANY),
                  pl.BlockSpec(memory_space=pl.ANY)],
        out_specs=pl.BlockSpec(memory_space=pl.ANY),
        scratch_shapes=[pltpu.VMEM((2, tm, D_MODEL), F32), pltpu.SemaphoreType.DMA((2,)),
                        pltpu.SemaphoreType.DMA(())],
    )
    return pl.pallas_call(
        _dispatch_kernel,
        grid_spec=grid_spec,
        out_shape=jax.ShapeDtypeStruct((MOE_ROWS, D_MODEL), F32),
        input_output_aliases={3: 0},
        compiler_params=_params(("arbitrary",), 32),
        name="moe_dispatch",
    )(pos0, pos1, h2, jnp.zeros((MOE_ROWS, D_MODEL), F32))


def _moe_ffn_kernel(te_ref, valid_ref, x_ref, wg_ref, wu_ref, wd_ref, o_ref, acc_ref):
    i = pl.program_id(0)
    j = pl.program_id(1)
    last = j == pl.num_programs(1) - 1

    @pl.when(valid_ref[i] == 1)
    def _():
        _swiglu_accumulate(x_ref[...].astype(BF16), wg_ref[0, 0], wu_ref[0, 0], wd_ref[0, 0].astype(BF16),
                           acc_ref, j)

        @pl.when(last)
        def _():
            o_ref[...] = acc_ref[...]

    @pl.when(jnp.logical_and(valid_ref[i] == 0, last))
    def _():
        o_ref[...] = jnp.zeros_like(o_ref)


def _moe_ffn(m, tile_expert, tile_valid, xs, wg, wu, wd):
    tm = TM_FFN
    row = pl.BlockSpec((tm, D_MODEL), lambda i, j, te, va: (i, 0))
    grid_spec = pltpu.PrefetchScalarGridSpec(
        num_scalar_prefetch=2,
        grid=(N_MOE_TILES, D_FF // TF),
        in_specs=[row,
                  pl.BlockSpec((1, 1, D_MODEL, TF), lambda i, j, te, va: (m, te[i], 0, j * va[i])),
                  pl.BlockSpec((1, 1, D_MODEL, TF), lambda i, j, te, va: (m, te[i], 0, j * va[i])),
                  pl.BlockSpec((1, 1, TF, D_MODEL), lambda i, j, te, va: (m, te[i], j * va[i], 0))],
        out_specs=row,
        scratch_shapes=[pltpu.VMEM((tm, D_MODEL), F32)],
    )
    return pl.pallas_call(
        _moe_ffn_kernel,
        grid_spec=grid_spec,
        out_shape=jax.ShapeDtypeStruct((MOE_ROWS, D_MODEL), F32),
        compiler_params=_params(("arbitrary", "arbitrary"), 56),
        name="ffn_moe",
    )(tile_expert, tile_valid, xs, wg, wu, wd)


def _moe_combine_kernel(p0_ref, p1_ref, x_ref, g2_ref, wt_ref, y_ref, o_ref, buf_ref, sem):
    tm = TM_COMBINE
    step = pl.program_id(0)
    slot = step % 2

    def gather(tile, sl):
        base = tile * tm

        for r in range(tm):
            _row_copy(y_ref, p0_ref[base + r], buf_ref.at[sl, 0], r, sem.at[sl]).start(priority=r % 2)
            _row_copy(y_ref, p1_ref[base + r], buf_ref.at[sl, 1], r, sem.at[sl]).start(priority=(r + 1) % 2)

    @pl.when(step == 0)
    def _():
        gather(step, slot)

    @pl.when(step + 1 < pl.num_programs(0))
    def _():
        gather(step + 1, 1 - slot)

    for k in range(2):
        pltpu.make_async_copy(y_ref.at[pl.ds(0, tm)], buf_ref.at[slot, k], sem.at[slot]).wait()
    w = wt_ref[...]
    f = w[:, 0:1] * buf_ref[slot, 0] + w[:, 1:2] * buf_ref[slot, 1]
    o_ref[...] = x_ref[...] + g2_ref[0] * f


def _moe_combine(pos0, pos1, x, gate2, wts, y):
    tm = TM_COMBINE
    per_batch = SEQ // tm
    row = pl.BlockSpec((tm, D_MODEL), lambda i, p0, p1: (i, 0))
    grid_spec = pltpu.PrefetchScalarGridSpec(
        num_scalar_prefetch=2,
        grid=(TOKENS // tm,),
        in_specs=[row,
                  pl.BlockSpec((1, 1, D_MODEL), lambda i, p0, p1: (i // per_batch, 0, 0)),
                  pl.BlockSpec((tm, LANES), lambda i, p0, p1: (i, 0)),
                  pl.BlockSpec(memory_space=pl.ANY)],
        out_specs=row,
        scratch_shapes=[pltpu.VMEM((2, 2, tm, D_MODEL), F32), pltpu.SemaphoreType.DMA((2,))],
    )
    return pl.pallas_call(
        _moe_combine_kernel,
        grid_spec=grid_spec,
        out_shape=jax.ShapeDtypeStruct((TOKENS, D_MODEL), F32),
        compiler_params=_params(("arbitrary",), 32),
        name="moe_combine",
    )(pos0, pos1, x, gate2, wts, y)


def _prep_w_in(w_in):
    cq_ckv = w_in[..., 0:Q_RANK + KV_RANK]
    kr = w_in[..., Q_RANK + KV_RANK:Q_RANK + KV_RANK + ROPE]
    rest = w_in[..., Q_RANK + KV_RANK + ROPE:]
    x1, x2 = kr[..., :ROPE // 2], kr[..., ROPE // 2:]
    z = lambda n: jnp.zeros(kr.shape[:-1] + (n,), w_in.dtype)
    kr_a = jnp.concatenate([z(NOPE), x1, x2, z(HEAD_TILE - QK_DIM)], axis=-1)
    kr_b = jnp.concatenate([z(NOPE), -x2, x1, z(HEAD_TILE - QK_DIM)], axis=-1)
    return jnp.concatenate([cq_ckv, kr_a, kr_b, rest], axis=-1).astype(BF16)


def _prep_w_uq(w_uq):
    w = w_uq.reshape(DEPTH, Q_RANK, N_HEADS, QK_DIM)
    nope, x1, x2 = w[..., :NOPE], w[..., NOPE:NOPE + ROPE // 2], w[..., NOPE + ROPE // 2:]
    z = lambda n: jnp.zeros(w.shape[:-1] + (n,), w.dtype)
    qa = jnp.concatenate([nope, x1, x2, z(HEAD_TILE - QK_DIM)], axis=-1)
    qb = jnp.concatenate([z(NOPE), -x2, x1, z(HEAD_TILE - QK_DIM)], axis=-1)
    flat = lambda a: a.reshape(DEPTH, Q_RANK, N_HEADS * HEAD_TILE)
    return jnp.concatenate([flat(qa), flat(qb)], axis=-1).astype(BF16)


def _prep_w_ukv(w_ukv):
    w = w_ukv.reshape(DEPTH, KV_RANK, N_HEADS, NOPE + V_DIM)
    k_nope, v = w[..., :NOPE], w[..., NOPE:]
    zk = jnp.zeros(k_nope.shape[:-1] + (HEAD_TILE - NOPE,), w.dtype)
    ka = jnp.concatenate([k_nope, zk], axis=-1)
    zv = jnp.zeros_like(v)
    even = (jnp.arange(N_HEADS) % 2 == 0)[None, None, :, None]
    vp = jnp.concatenate([jnp.where(even, v, zv), jnp.where(even, zv, v)], axis=-1)
    flat = lambda a: a.reshape(DEPTH, KV_RANK, N_HEADS * HEAD_TILE)
    return jnp.concatenate([flat(ka), flat(vp)], axis=-1).astype(BF16)


def _head_gain(g, scale):
    pad = jnp.zeros((DEPTH, HEAD_TILE - QK_DIM), g.dtype)
    return (jnp.concatenate([g, pad], axis=-1) * scale).reshape(DEPTH, 1, HEAD_TILE)


def _rope_tables(positions):
    inv_freq = 1.0 / (10000.0 ** (jnp.arange(0, ROPE, 2, dtype=F32) / ROPE))
    ang = positions.astype(F32).reshape(TOKENS, 1) * inv_freq
    cos, sin = jnp.cos(ang), jnp.sin(ang)
    cos_t = jnp.concatenate([jnp.ones((TOKENS, NOPE), F32), cos, cos,
                             jnp.zeros((TOKENS, HEAD_TILE - QK_DIM), F32)], axis=-1)
    sin_t = jnp.concatenate([jnp.zeros((TOKENS, NOPE), F32), sin, sin,
                             jnp.zeros((TOKENS, HEAD_TILE - QK_DIM), F32)], axis=-1)
    return cos_t, sin_t


def _moe_plan(meta, cnt):
    e1, e2, r1, r2 = meta[:, 0], meta[:, 1], meta[:, 2], meta[:, 3]
    counts = cnt[0, :N_EXPERTS].astype(jnp.int32)
    padded = ((counts + TM_FFN - 1) // TM_FFN) * TM_FFN
    ends = jnp.cumsum(padded)
    starts = ends - padded
    experts = jnp.arange(N_EXPERTS, dtype=jnp.int32)
    start_of = lambda e: jnp.sum(jnp.where(e[:, None] == experts[None, :], starts[None, :], 0), axis=1)
    pos0 = start_of(e1) + r1
    pos1 = start_of(e2) + r2
    tile_start = jnp.arange(N_MOE_TILES, dtype=jnp.int32) * TM_FFN
    tile_expert = jnp.minimum(jnp.sum(tile_start[:, None] >= ends[None, :], axis=1), N_EXPERTS - 1)
    tile_valid = (tile_start < ends[-1]).astype(jnp.int32)
    return pos0.astype(jnp.int32), pos1.astype(jnp.int32), tile_expert.astype(jnp.int32), tile_valid


def kernel(x, c, positions, ada_w, ada_b, norm1_g, norm2_g, w_in, q_norm_g, kv_norm_g, w_uq, w_ukv, q_head_g,
           k_head_g, w_o_mla, conv_w, conv_b, lru_gate_w, lru_gate_b, lru_a_param, w_o_lru, w_out, ffn_w_gate,
           ffn_w_up, ffn_w_down, moe_router, moe_w_gate, moe_w_up, moe_w_down):
    cos_t, sin_t = _rope_tables(positions)
    mod = _modulation(c, ada_w, ada_b).reshape(DEPTH, BATCH, 6, 1, D_MODEL)

    vec = lambda a: a.reshape(DEPTH, 1, a.shape[-1])
    w_in_p = _prep_w_in(w_in)
    wq_p = _prep_w_uq(w_uq)
    wkv_p = _prep_w_ukv(w_ukv)
    hq = _head_gain(q_head_g, 0.5 * TWO_LOG2E * QK_DIM ** -0.5)
    hk = _head_gain(k_head_g, 1.0)
    wg_p = (0.5 * lru_gate_w).transpose(0, 3, 4, 1, 2, 5).reshape(
        DEPTH, N_LRU_BLOCKS, LRU_BLOCK, 4 * LRU_BLOCK).astype(BF16)
    bg_p = (0.5 * lru_gate_b).reshape(DEPTH, 2, 2, N_LRU_BLOCKS, LRU_BLOCK).transpose(0, 3, 1, 2, 4).reshape(
        DEPTH, N_LRU_BLOCKS, 1, 4 * LRU_BLOCK)
    ap_p = lru_a_param.reshape(DEPTH, 2, N_LRU_BLOCKS, LRU_BLOCK).transpose(0, 2, 1, 3).reshape(
        DEPTH, N_LRU_BLOCKS, 1, 2 * LRU_BLOCK)
    conv_b_p = conv_b.reshape(DEPTH, 1, LRU_W)
    wol, wom, wout = w_o_lru.astype(BF16), w_o_mla.astype(BF16), w_out.astype(BF16)
    fg, fu, fd = ffn_w_gate.astype(BF16), ffn_w_up.astype(BF16), ffn_w_down.astype(BF16)
    mg, mu, md = moe_w_gate.astype(BF16), moe_w_up.astype(BF16), moe_w_down
    router_hi = moe_router.astype(BF16)
    router_lo = (moe_router - router_hi.astype(F32)).astype(BF16)
    router = jnp.pad(jnp.concatenate([router_hi, router_lo], axis=-1),
                     ((0, 0), (0, 0), (0, LANES - 2 * N_EXPERTS)))
    n1, n2, gq, gkv = vec(norm1_g), vec(norm2_g), vec(q_norm_g), vec(kv_norm_g)

    xt = x.reshape(TOKENS, D_MODEL)
    for l in range(DEPTH):
        shift1, scale1, gate1, shift2, scale2, gate2 = [mod[l, :, k] for k in range(6)]
        q, k, v, u, y_gate, gb = _in_projection(l, xt, shift1, scale1, n1, w_in_p, cos_t, sin_t, gq, gkv,
                                                wq_p, wkv_p, hq, hk)
        attn = _attention(q, k, v).reshape(TOKENS, N_HEADS * V_DIM)
        h_lru = _lru_branch(l, u.reshape(BATCH, SEQ, LRU_W), conv_w, conv_b_p, wg_p, bg_p, ap_p)
        h_lru = h_lru.reshape(TOKENS, LRU_W)
        m = l // 2
        if l % 2 == 0:
            xt, h2 = _mix(l, h_lru, y_gate, attn, gb, xt, gate1, shift2, scale2, n2, wol, wom, wout)
            xt = _dense_ffn(m, h2, xt, gate2, fg, fu, fd)
        else:
            xt, h2, meta, wts, cnt = _mix(l, h_lru, y_gate, attn, gb, xt, gate1, shift2, scale2, n2, wol, wom,
                                          wout, router=router)
            pos0, pos1, tile_expert, tile_valid = _moe_plan(meta, cnt)
            xs = _dispatch(pos0, pos1, h2)
            y = _moe_ffn(m, tile_expert, tile_valid, xs, mg, mu, md)
            xt = _moe_combine(pos0, pos1, xt, gate2, wts, y)
    return xt.reshape(BATCH, SEQ, D_MODEL)
```

```python
import functools

import jax
import jax.numpy as jnp
from jax import lax
from jax.experimental import pallas as pl
from jax.experimental.pallas import tpu as pltpu

F32 = jnp.float32
BF16 = jnp.bfloat16

D_MODEL = 1024
BATCH = 8
SEQ = 2048
TOKENS = BATCH * SEQ
DEPTH = 4
N_HEADS = 8
NOPE = 64
ROPE = 32
QK_DIM = NOPE + ROPE
V_DIM = 64
Q_RANK = 256
KV_RANK = 256
LRU_W = D_MODEL
N_LRU_BLOCKS = 8
LRU_BLOCK = LRU_W // N_LRU_BLOCKS
RG_LRU_C = 8.0
D_FF = 2816
N_EXPERTS = 8
N_MOE_LAYERS = DEPTH // 2
EPS = 1e-6

LANES = 128
SUBLANES = 8
HEAD_TILE = LANES

MLA_COLS = Q_RANK + KV_RANK + 2 * HEAD_TILE
IN_COLS = MLA_COLS + 2 * LRU_W + 2 * D_MODEL

TM_PROJ = 512
TQ = 512
TM_FFN = 512
TF = D_FF // 2
N_MOE_TILES = (2 * TOKENS) // TM_FFN + N_EXPERTS
MOE_ROWS = N_MOE_TILES * TM_FFN
TM_DISPATCH = 512
TM_COMBINE = 256

MIB = 1024 * 1024


def _params(sem, vmem_mib):
    return pltpu.CompilerParams(dimension_semantics=sem, vmem_limit_bytes=vmem_mib * MIB)


def _dot(a, b):
    return jnp.dot(a, b, preferred_element_type=F32)


def _split_bf16(a):
    hi = a.astype(BF16)
    lo = (a - hi.astype(F32)).astype(BF16)
    return hi, lo


def _rms(v, inv_n):
    return v * lax.rsqrt(jnp.sum(v * v, axis=-1, keepdims=True) * inv_n + EPS)


def _layer_spec(a, l):
    nd = a.ndim - 1
    return pl.BlockSpec((1,) + a.shape[1:], lambda i: (l,) + (0,) * nd)


TN_MOD = 1536


def _mod_kernel(c_ref, w_ref, b_ref, o_ref):
    c = c_ref[...]
    ch, cl = _split_bf16(c * jax.nn.sigmoid(c))
    wh, wl = _split_bf16(w_ref[0])
    o_ref[0] = _dot(ch, wh) + (_dot(ch, wl) + _dot(cl, wh)) + b_ref[0]


def _modulation(c, ada_w, ada_b):
    n = 6 * D_MODEL
    return pl.pallas_call(
        _mod_kernel,
        grid=(DEPTH, n // TN_MOD),
        in_specs=[
            pl.BlockSpec((BATCH, D_MODEL), lambda l, j: (0, 0)),
            pl.BlockSpec((1, D_MODEL, TN_MOD), lambda l, j: (l, 0, j)),
            pl.BlockSpec((1, 1, TN_MOD), lambda l, j: (l, 0, j)),
        ],
        out_specs=pl.BlockSpec((1, BATCH, TN_MOD), lambda l, j: (l, 0, j)),
        out_shape=jax.ShapeDtypeStruct((DEPTH, BATCH, n), F32),
        compiler_params=_params(("arbitrary", "arbitrary"), 40),
        name="adaln_mod",
    )(c, ada_w, ada_b.reshape(DEPTH, 1, n))


def _batch_vec_spec(tm):
    per_batch = SEQ // tm
    return pl.BlockSpec((1, 1, D_MODEL), lambda i: (i // per_batch, 0, 0))


def _resident_spec(a, l):
    nd = a.ndim - 1
    return pl.BlockSpec((1,) + a.shape[1:], lambda i: (l,) + (0,) * nd, pipeline_mode=pl.Buffered(1))


def _inproj_kernel(x_ref, sh_ref, sc_ref, g_ref, w_ref, cos_ref, sin_ref, gq_ref, gkv_ref, wq_ref, wkv_ref,
                   hq_ref, hk_ref, q_ref, k_ref, v_ref, u_ref, y_ref, gb_ref):
    x = x_ref[...]
    h = _rms(x, 1.0 / D_MODEL) * g_ref[0]
    h = (h * (1.0 + sc_ref[0]) + sh_ref[0]).astype(BF16)
    c0, c1, c2 = MLA_COLS, MLA_COLS + LRU_W, MLA_COLS + 2 * LRU_W
    m = _dot(h, w_ref[0, :, 0:c0])
    u_ref[...] = _dot(h, w_ref[0, :, c0:c1])
    y_ref[...] = _dot(h, w_ref[0, :, c1:c2]).astype(BF16)
    gb_ref[...] = _dot(h, w_ref[0, :, c2:IN_COLS]).astype(BF16)

    cos = cos_ref[...]
    sin = sin_ref[...]
    cq = m[:, 0:Q_RANK]
    ckv = m[:, Q_RANK:Q_RANK + KV_RANK]
    kr_a = m[:, Q_RANK + KV_RANK:Q_RANK + KV_RANK + HEAD_TILE]
    kr_b = m[:, Q_RANK + KV_RANK + HEAD_TILE:MLA_COLS]
    cqn = (_rms(cq, 1.0 / Q_RANK) * gq_ref[0]).astype(BF16)
    ckvn = (_rms(ckv, 1.0 / KV_RANK) * gkv_ref[0]).astype(BF16)
    qq = _dot(cqn, wq_ref[0])
    kv = _dot(ckvn, wkv_ref[0])
    k_pe = kr_a * cos + kr_b * sin
    nq = N_HEADS * HEAD_TILE
    lane = lax.broadcasted_iota(jnp.int32, (TM_PROJ, HEAD_TILE), 1)
    for hd in range(N_HEADS):
        lo, hi = hd * HEAD_TILE, (hd + 1) * HEAD_TILE
        qh = qq[:, lo:hi] * cos + qq[:, nq + lo:nq + hi] * sin
        q_ref[0, hd] = (_rms(qh, 1.0 / QK_DIM) * hq_ref[0]).astype(BF16)
        kh = kv[:, lo:hi] + k_pe
        k_ref[0, hd] = (_rms(kh, 1.0 / QK_DIM) * hk_ref[0]).astype(BF16)
        v_ref[0, hd] = jnp.where(lane == V_DIM * (1 - hd % 2), 1.0, kv[:, nq + lo:nq + hi]).astype(BF16)


def _in_projection(l, x, shift, scale, gain, w, cos_t, sin_t, gq, gkv, wq, wkv, hq, hk):
    tm = TM_PROJ
    per_batch = SEQ // tm
    row = lambda n: pl.BlockSpec((tm, n), lambda i: (i, 0))
    head = pl.BlockSpec((1, N_HEADS, tm, HEAD_TILE), lambda i: (i // per_batch, 0, i % per_batch, 0))
    head_shape = jax.ShapeDtypeStruct((BATCH, N_HEADS, SEQ, HEAD_TILE), BF16)
    params = [gq, gkv, wq, wkv, hq, hk]
    return pl.pallas_call(
        _inproj_kernel,
        grid=(TOKENS // tm,),
        in_specs=[row(D_MODEL), _batch_vec_spec(tm), _batch_vec_spec(tm), _layer_spec(gain, l),
                  _resident_spec(w, l), row(HEAD_TILE), row(HEAD_TILE)] + [_resident_spec(p, l) for p in params],
        out_specs=[head, head, head, row(LRU_W), row(LRU_W), row(2 * D_MODEL)],
        out_shape=[
            head_shape, head_shape, head_shape,
            jax.ShapeDtypeStruct((TOKENS, LRU_W), F32),
            jax.ShapeDtypeStruct((TOKENS, LRU_W), BF16),
            jax.ShapeDtypeStruct((TOKENS, 2 * D_MODEL), BF16),
        ],
        compiler_params=_params(("arbitrary",), 56),
        name="in_proj",
    )(x, shift, scale, gain, w, cos_t, sin_t, *params)


def _attn_kernel(q_ref, k_ref, v_ref, o_ref):
    lane = lax.broadcasted_iota(jnp.int32, (TQ, LANES), 1)
    for pair in range(N_HEADS // 2):
        acc = None
        for j in range(2):
            h = 2 * pair + j
            s = lax.dot_general(q_ref[0, h], k_ref[0, h], (((1,), (1,)), ((), ())),
                                preferred_element_type=F32)
            p = jnp.exp2(s - jnp.max(s, axis=-1, keepdims=True)).astype(BF16)
            o = _dot(p, v_ref[0, h])
            ones_lane = V_DIM * (1 - j)
            inv = 1.0 / o[:, ones_lane:ones_lane + 1]
            mine = (lane < V_DIM) if j == 0 else (lane >= V_DIM)
            o = jnp.where(mine, o * inv, 0.0)
            acc = o if acc is None else acc + o
        o_ref[0, :, pair * LANES:(pair + 1) * LANES] = acc.astype(BF16)


def _attention(q, k, v):
    nq = SEQ // TQ
    kv_spec = pl.BlockSpec((1, N_HEADS, SEQ, HEAD_TILE), lambda b, i: (b, 0, 0, 0))
    return pl.pallas_call(
        _attn_kernel,
        grid=(BATCH, nq),
        in_specs=[pl.BlockSpec((1, N_HEADS, TQ, HEAD_TILE), lambda b, i: (b, 0, i, 0)), kv_spec, kv_spec],
        out_specs=pl.BlockSpec((1, TQ, N_HEADS * V_DIM), lambda b, i: (b, i, 0)),
        out_shape=jax.ShapeDtypeStruct((BATCH, SEQ, N_HEADS * V_DIM), BF16),
        compiler_params=_params(("arbitrary", "arbitrary"), 48),
        name="attention",
    )(q, k, v)


CHUNK = SEQ // SUBLANES
PITCH = CHUNK + 4
CONV_BEFORE, CONV_AFTER = 2, 1
TWO_LOG2E = 2.0 * 1.4426950408889634


def _shift_rows(v, row, down):
    if down:
        return jnp.where(row == 0, 0.0, pltpu.roll(v, 1, 0))
    return jnp.where(row == SUBLANES - 1, 0.0, pltpu.roll(v, SUBLANES - 1, 0))


def _scan_sublanes(a, b, row, reverse):
    for s in (1, 2, 4):
        shift = SUBLANES - s if reverse else s
        a_s = pltpu.roll(a, shift, 0)
        b_s = pltpu.roll(b, shift, 0)
        keep = (row < SUBLANES - s) if reverse else (row >= s)
        b = jnp.where(keep, a * b_s + b, b)
        a = jnp.where(keep, a * a_s, a)
    return b


def _lru_kernel(u_hbm, cw_ref, cb_ref, wg_ref, bg_ref, ap_ref, h_hbm,
                uin_ref, hout_ref, up_ref, uc_ref, g_ref, a_ref, b_ref, h_ref, p_ref, in_sem, out_sem):
    n_steps = pl.num_programs(0) * N_LRU_BLOCKS
    step = pl.program_id(0) * N_LRU_BLOCKS + pl.program_id(1)
    slot = step % 2

    def chunk_copies(s, sl, inbound):
        bb = s // N_LRU_BLOCKS
        col = pl.multiple_of((s % N_LRU_BLOCKS) * LRU_BLOCK, LRU_BLOCK)
        out = []
        for j in range(SUBLANES):
            hbm = (u_hbm if inbound else h_hbm).at[bb, pl.ds(j * CHUNK, CHUNK), pl.ds(col, LRU_BLOCK)]
            if inbound:
                out.append(pltpu.make_async_copy(hbm, uin_ref.at[sl, pl.ds(j * PITCH, CHUNK), :], in_sem.at[sl]))
            else:
                out.append(pltpu.make_async_copy(hout_ref.at[sl, pl.ds(j * PITCH, CHUNK), :], hbm, out_sem.at[sl]))
        return out

    @pl.when(step == 0)
    def _():
        for cp in chunk_copies(step, slot, True):
            cp.start()

    @pl.when(step + 1 < n_steps)
    def _():
        for cp in chunk_copies(step + 1, 1 - slot, True):
            cp.start()

    for cp in chunk_copies(step, slot, True):
        cp.wait()

    row = lax.broadcasted_iota(jnp.int32, (SUBLANES, LRU_BLOCK), 0)

    def tile_rows(k):
        return slice(k * SUBLANES, (k + 1) * SUBLANES)

    for k in range(CHUNK):
        up_ref[tile_rows(k + CONV_BEFORE), :] = uin_ref[slot, pl.ds(k, SUBLANES, stride=PITCH), :]
    for i in range(CONV_BEFORE):
        src = up_ref[(CHUNK + i) * SUBLANES:(CHUNK + i + 1) * SUBLANES, :]
        up_ref[i * SUBLANES:(i + 1) * SUBLANES, :] = _shift_rows(src, row, True)
    for i in range(CONV_AFTER):
        src = up_ref[(CONV_BEFORE + i) * SUBLANES:(CONV_BEFORE + i + 1) * SUBLANES, :]
        dst = (CONV_BEFORE + CHUNK + i) * SUBLANES
        up_ref[dst:dst + SUBLANES, :] = _shift_rows(src, row, False)

    uc = cb_ref[0]
    for tap in range(CONV_BEFORE + CONV_AFTER + 1):
        uc = uc + up_ref[tap * SUBLANES:tap * SUBLANES + SEQ, :] * cw_ref[0, tap:tap + 1, :]
    uc_ref[...] = uc
    g_ref[...] = _dot(uc.astype(BF16), wg_ref[0, 0]) + bg_ref[0, 0]

    sp = jax.nn.softplus(ap_ref[0, 0])
    u_half = 0.5 * uc_ref[...]
    for d in range(2):
        quarter_coef = (-0.25 * RG_LRU_C) * sp[:, d * LRU_BLOCK:(d + 1) * LRU_BLOCK]
        g_r = g_ref[:, 2 * d * LRU_BLOCK:(2 * d + 1) * LRU_BLOCK]
        g_i = g_ref[:, (2 * d + 1) * LRU_BLOCK:(2 * d + 2) * LRU_BLOCK]
        x = quarter_coef + quarter_coef * jnp.tanh(g_r)
        a = jnp.exp2(TWO_LOG2E * x)
        s = -jnp.tanh(x)
        mult = (1.0 + a) * jnp.where(s > 0.0, s * lax.rsqrt(s), 0.0)
        gated = u_half + u_half * jnp.tanh(g_i)
        a_ref[d] = a
        b_ref[d] = mult * gated
        first = slice(0, SUBLANES) if d == 0 else slice(SEQ - SUBLANES, SEQ)
        first_row = 0 if d == 0 else SUBLANES - 1
        g_first = u_half[first, :] + u_half[first, :] * jnp.tanh(g_i[first, :])
        b_ref[d, first, :] = jnp.where(row == first_row, g_first, b_ref[d, first, :])

    zeros = jnp.zeros((SUBLANES, LRU_BLOCK), F32)
    ones = jnp.ones((SUBLANES, LRU_BLOCK), F32)
    state = [(zeros, ones), (zeros, ones)]
    for m in range(CHUNK // 2):
        for d in range(2):
            k0, k1 = (2 * m, 2 * m + 1) if d == 0 else (CHUNK - 1 - 2 * m, CHUNK - 2 - 2 * m)
            r0, r1 = tile_rows(k0), tile_rows(k1)
            a0, b0 = a_ref[d, r0, :], b_ref[d, r0, :]
            a1, b1 = a_ref[d, r1, :], b_ref[d, r1, :]
            a01 = a1 * a0
            b01 = a1 * b0 + b1
            h, p = state[d]
            h_ref[d, r0, :] = a0 * h + b0
            p_ref[d, r0, :] = a0 * p
            h = a01 * h + b01
            p = a01 * p
            h_ref[d, r1, :] = h
            p_ref[d, r1, :] = p
            state[d] = (h, p)
    (hf, af), (hb, ab) = state
    carry_f = _shift_rows(_scan_sublanes(af, hf, row, False), row, True)
    carry_b = _shift_rows(_scan_sublanes(ab, hb, row, True), row, False)

    @pl.when(step >= 2)
    def _():
        for cp in chunk_copies(step, slot, False):
            cp.wait()

    for k in range(CHUNK):
        r = tile_rows(k)
        h = (h_ref[0, r, :] + p_ref[0, r, :] * carry_f) + (h_ref[1, r, :] + p_ref[1, r, :] * carry_b)
        hout_ref[slot, pl.ds(k, SUBLANES, stride=PITCH), :] = h
    for cp in chunk_copies(step, slot, False):
        cp.start()

    @pl.when(step == n_steps - 1)
    def _():
        for cp in chunk_copies(step, 1 - slot, False) + chunk_copies(step, slot, False):
            cp.wait()


def _lru_branch(l, u, conv_w, conv_b, wg, bg, a_param):
    batch = u.shape[0]
    vm = lambda *shape: pltpu.VMEM(shape, F32)
    stage = (2, SUBLANES * PITCH, LRU_BLOCK)
    return pl.pallas_call(
        _lru_kernel,
        grid=(batch, N_LRU_BLOCKS),
        in_specs=[
            pl.BlockSpec(memory_space=pl.ANY),
            pl.BlockSpec((1, 4, LRU_BLOCK), lambda b, n: (l, 0, n)),
            pl.BlockSpec((1, 1, LRU_BLOCK), lambda b, n: (l, 0, n)),
            pl.BlockSpec((1, 1, LRU_BLOCK, 4 * LRU_BLOCK), lambda b, n: (l, n, 0, 0)),
            pl.BlockSpec((1, 1, 1, 4 * LRU_BLOCK), lambda b, n: (l, n, 0, 0)),
            pl.BlockSpec((1, 1, 1, 2 * LRU_BLOCK), lambda b, n: (l, n, 0, 0)),
        ],
        out_specs=pl.BlockSpec(memory_space=pl.ANY),
        out_shape=jax.ShapeDtypeStruct((batch, SEQ, LRU_W), F32),
        scratch_shapes=[vm(*stage), vm(*stage),
                        vm(SEQ + (CONV_BEFORE + CONV_AFTER) * SUBLANES, LRU_BLOCK), vm(SEQ, LRU_BLOCK),
                        vm(SEQ, 4 * LRU_BLOCK)] + [vm(2, SEQ, LRU_BLOCK)] * 4 + [
                        pltpu.SemaphoreType.DMA((2,)), pltpu.SemaphoreType.DMA((2,))],
        compiler_params=_params(("arbitrary", "arbitrary"), 40),
        name="rg_lru",
    )(u, conv_w, conv_b, wg, bg, a_param)


def _gelu_tanh(x):
    return 0.5 * x * (1.0 + jnp.tanh(0.7978845608028654 * (x + 0.044715 * x * x * x)))


def _route(lg, meta_ref, wt_ref, cnt_ref, carry_ref):
    tm = lg.shape[0]

    @pl.when(pl.program_id(0) == 0)
    def _():
        carry_ref[...] = jnp.zeros_like(carry_ref)

    lane = lax.broadcasted_iota(jnp.int32, (tm, LANES), 1)
    lane_f = lane.astype(F32)
    neg = jnp.float32(-jnp.inf)
    lg = jnp.where(lane < N_EXPERTS, lg, neg)
    m1 = jnp.max(lg, axis=-1, keepdims=True)
    i1 = jnp.min(jnp.where(lg == m1, lane_f, float(LANES)), axis=-1, keepdims=True).astype(jnp.int32)
    lg2 = jnp.where(lane == i1, neg, lg)
    m2 = jnp.max(lg2, axis=-1, keepdims=True)
    i2 = jnp.min(jnp.where(lg2 == m2, lane_f, float(LANES)), axis=-1, keepdims=True).astype(jnp.int32)
    e = jnp.exp(m2 - m1)
    w1 = 1.0 / (1.0 + e)
    w2 = e * w1
    sel = jnp.where(lane == i1, 1.0, jnp.where(lane == i2, 1.0, 0.0))
    r_i = lax.broadcasted_iota(jnp.int32, (tm, tm), 0)
    c_i = lax.broadcasted_iota(jnp.int32, (tm, tm), 1)
    tri = jnp.where(r_i > c_i, 1.0, 0.0).astype(BF16)
    cum = _dot(tri, sel.astype(BF16)) + carry_ref[0:1, :]
    r1 = jnp.sum(jnp.where(lane == i1, cum, 0.0), axis=-1, keepdims=True).astype(jnp.int32)
    r2 = jnp.sum(jnp.where(lane == i2, cum, 0.0), axis=-1, keepdims=True).astype(jnp.int32)
    total = carry_ref[0:1, :] + jnp.sum(sel, axis=0, keepdims=True)
    carry_ref[...] = jnp.broadcast_to(total, carry_ref.shape)
    cnt_ref[...] = jnp.broadcast_to(total, cnt_ref.shape)
    meta_ref[...] = jnp.where(lane == 0, i1, jnp.where(lane == 1, i2, jnp.where(lane == 2, r1,
                              jnp.where(lane == 3, r2, 0))))
    wt_ref[...] = jnp.where(lane == 0, w1, jnp.where(lane == 1, w2, 0.0))


def _mix_kernel(*refs, moe):
    if moe:
        (h_ref, y_ref, at_ref, gb_ref, x_ref, g1_ref, sh_ref, sc_ref, n2_ref, wol_ref, wom_ref, wout_ref,
         wr_ref, xo_ref, h2_ref, meta_ref, wt_ref, cnt_ref, carry_ref) = refs
    else:
        (h_ref, y_ref, at_ref, gb_ref, x_ref, g1_ref, sh_ref, sc_ref, n2_ref, wol_ref, wom_ref, wout_ref,
         xo_ref, h2_ref) = refs
    hg = (h_ref[...] * _gelu_tanh(y_ref[...].astype(F32))).astype(BF16)
    y_lru = _dot(hg, wol_ref[0])
    y_mla = _dot(at_ref[...], wom_ref[0])
    g_lru = jax.nn.sigmoid(gb_ref[:, 0:D_MODEL].astype(F32))
    g_mla = jax.nn.sigmoid(gb_ref[:, D_MODEL:2 * D_MODEL].astype(F32))
    z = (g_lru * y_lru + g_mla * y_mla).astype(BF16)
    xn = x_ref[...] + g1_ref[0] * _dot(z, wout_ref[0])
    xo_ref[...] = xn
    h2 = _rms(xn, 1.0 / D_MODEL) * n2_ref[0]
    h2 = h2 * (1.0 + sc_ref[0]) + sh_ref[0]
    if moe:
        h2_ref[...] = h2
        hh, hl = _split_bf16(h2)
        parts = _dot(hh, wr_ref[0]) + _dot(hl, wr_ref[0])
        _route(parts + pltpu.roll(parts, LANES - N_EXPERTS, 1), meta_ref, wt_ref, cnt_ref, carry_ref)
    else:
        h2_ref[...] = h2.astype(BF16)


def _mix(l, h_lru, y_gate, attn, gb, x, gate1, shift2, scale2, n2, wol, wom, wout, router=None):
    tm = TM_PROJ
    moe = router is not None
    row = lambda n: pl.BlockSpec((tm, n), lambda i: (i, 0))
    vec = _batch_vec_spec(tm)
    ins = [h_lru, y_gate, attn, gb, x, gate1, shift2, scale2, n2, wol, wom, wout]
    in_specs = [row(LRU_W), row(LRU_W), row(N_HEADS * V_DIM), row(2 * D_MODEL), row(D_MODEL), vec, vec, vec,
                _layer_spec(n2, l), _layer_spec(wol, l), _layer_spec(wom, l), _layer_spec(wout, l)]
    out_specs = [row(D_MODEL), row(D_MODEL)]
    out_shape = [jax.ShapeDtypeStruct((TOKENS, D_MODEL), F32),
                 jax.ShapeDtypeStruct((TOKENS, D_MODEL), F32 if moe else BF16)]
    scratch = []
    if moe:
        ins.append(router)
        in_specs.append(_layer_spec(router, l // 2))
        out_specs += [row(LANES), row(LANES), pl.BlockSpec((SUBLANES, LANES), lambda i: (0, 0))]
        out_shape += [jax.ShapeDtypeStruct((TOKENS, LANES), jnp.int32),
                      jax.ShapeDtypeStruct((TOKENS, LANES), F32),
                      jax.ShapeDtypeStruct((SUBLANES, LANES), F32)]
        scratch.append(pltpu.VMEM((SUBLANES, LANES), F32))
    return pl.pallas_call(
        functools.partial(_mix_kernel, moe=moe),
        grid=(TOKENS // tm,),
        in_specs=in_specs,
        out_specs=out_specs,
        out_shape=out_shape,
        scratch_shapes=scratch,
        compiler_params=_params(("arbitrary",), 48),
        name="mix_moe" if moe else "mix_dense",
    )(*ins)


def _swiglu_accumulate(h, wg, wu, wd, acc_ref, j, finish):
    a = _dot(h, wg)
    b = _dot(h, wu)
    part = _dot((a * jax.nn.sigmoid(a) * b).astype(BF16), wd)
    last = pl.num_programs(1) - 1

    @pl.when(j == 0)
    def _():
        acc_ref[...] = part

    @pl.when(jnp.logical_and(j > 0, j < last))
    def _():
        acc_ref[...] += part

    @pl.when(j == last)
    def _():
        finish(acc_ref[...] + part)


def _ffn_kernel(h_ref, x_ref, g2_ref, wg_ref, wu_ref, wd_ref, o_ref, acc_ref):
    def finish(total):
        o_ref[...] = x_ref[...] + g2_ref[0] * total

    _swiglu_accumulate(h_ref[...], wg_ref[0], wu_ref[0], wd_ref[0], acc_ref, pl.program_id(1), finish)


def _dense_ffn(m, h2, x, gate2, wg, wu, wd):
    tm = TM_FFN
    per_batch = SEQ // tm
    row = pl.BlockSpec((tm, D_MODEL), lambda i, j: (i, 0))
    return pl.pallas_call(
        _ffn_kernel,
        grid=(TOKENS // tm, D_FF // TF),
        in_specs=[row, row, pl.BlockSpec((1, 1, D_MODEL), lambda i, j: (i // per_batch, 0, 0)),
                  pl.BlockSpec((1, D_MODEL, TF), lambda i, j: (m, 0, j)),
                  pl.BlockSpec((1, D_MODEL, TF), lambda i, j: (m, 0, j)),
                  pl.BlockSpec((1, TF, D_MODEL), lambda i, j: (m, j, 0))],
        out_specs=row,
        out_shape=jax.ShapeDtypeStruct((TOKENS, D_MODEL), F32),
        scratch_shapes=[pltpu.VMEM((tm, D_MODEL), F32)],
        compiler_params=_params(("arbitrary", "arbitrary"), 56),
        name="ffn_dense",
    )(h2, x, gate2, wg, wu, wd)


def _row_copy(src_ref, src_row, dst_ref, dst_row, sem):
    return pltpu.make_async_copy(src_ref.at[pl.ds(src_row, 1)], dst_ref.at[pl.ds(dst_row, 1)], sem)


def _dispatch_kernel(p0_ref, p1_ref, pad_start_ref, pad_len_ref, h_hbm, xs_ref, buf_ref, zero_ref,
                     load_sem, row_sem, pad_sem):
    tm = TM_DISPATCH
    step = pl.program_id(0)
    n_steps = pl.num_programs(0)
    slot = step % 2
    base = step * tm

    def load(tile, sl):
        return pltpu.make_async_copy(h_hbm.at[pl.ds(tile * tm, tm)], buf_ref.at[sl], load_sem.at[sl])

    def drain_rows():
        for _ in range(2):
            pltpu.make_async_copy(buf_ref.at[0], xs_ref.at[pl.ds(0, tm)], row_sem).wait()

    def for_each_pad_row(fn):
        for e in range(N_EXPERTS):
            start = pad_start_ref[e]

            def body(i, c):
                fn(start + i)
                return c

            lax.fori_loop(0, pad_len_ref[e], body, 0)

    @pl.when(step == 0)
    def _():
        load(step, slot).start()
        zero_ref[...] = jnp.zeros_like(zero_ref)
        for_each_pad_row(lambda r: _row_copy(zero_ref, 0, xs_ref, r, pad_sem).start())

    @pl.when(step > 0)
    def _():
        drain_rows()

    @pl.when(step + 1 < n_steps)
    def _():
        load(step + 1, 1 - slot).start()

    load(step, slot).wait()
    src = buf_ref.at[slot]

    for r in range(tm):
        _row_copy(src, r, xs_ref, p0_ref[base + r], row_sem).start(priority=r % 2)
        _row_copy(src, r, xs_ref, p1_ref[base + r], row_sem).start(priority=(r + 1) % 2)

    @pl.when(step == n_steps - 1)
    def _():
        drain_rows()
        for_each_pad_row(lambda r: _row_copy(zero_ref, 0, xs_ref, r, pad_sem).wait())


def _dispatch(pos0, pos1, pad_start, pad_len, h2):
    tm = TM_DISPATCH
    grid_spec = pltpu.PrefetchScalarGridSpec(
        num_scalar_prefetch=4,
        grid=(TOKENS // tm,),
        in_specs=[pl.BlockSpec(memory_space=pl.ANY)],
        out_specs=pl.BlockSpec(memory_space=pl.ANY),
        scratch_shapes=[pltpu.VMEM((2, tm, D_MODEL), F32), pltpu.VMEM((SUBLANES, D_MODEL), F32),
                        pltpu.SemaphoreType.DMA((2,)), pltpu.SemaphoreType.DMA(()), pltpu.SemaphoreType.DMA(())],
    )
    return pl.pallas_call(
        _dispatch_kernel,
        grid_spec=grid_spec,
        out_shape=jax.ShapeDtypeStruct((MOE_ROWS, D_MODEL), F32),
        compiler_params=_params(("arbitrary",), 32),
        name="moe_dispatch",
    )(pos0, pos1, pad_start, pad_len, h2)


def _moe_ffn_kernel(te_ref, valid_ref, x_ref, wg_ref, wu_ref, wd_ref, o_ref, acc_ref):
    i = pl.program_id(0)
    j = pl.program_id(1)
    last = j == pl.num_programs(1) - 1

    @pl.when(valid_ref[i] == 1)
    def _():
        def finish(total):
            o_ref[...] = total

        _swiglu_accumulate(x_ref[...].astype(BF16), wg_ref[0, 0], wu_ref[0, 0], wd_ref[0, 0].astype(BF16),
                           acc_ref, j, finish)

    @pl.when(jnp.logical_and(valid_ref[i] == 0, last))
    def _():
        o_ref[...] = jnp.zeros_like(o_ref)


def _moe_ffn(m, tile_expert, tile_valid, xs, wg, wu, wd):
    tm = TM_FFN
    row = pl.BlockSpec((tm, D_MODEL), lambda i, j, te, va: (i, 0))
    grid_spec = pltpu.PrefetchScalarGridSpec(
        num_scalar_prefetch=2,
        grid=(N_MOE_TILES, D_FF // TF),
        in_specs=[row,
                  pl.BlockSpec((1, 1, D_MODEL, TF), lambda i, j, te, va: (m, te[i], 0, j * va[i])),
                  pl.BlockSpec((1, 1, D_MODEL, TF), lambda i, j, te, va: (m, te[i], 0, j * va[i])),
                  pl.BlockSpec((1, 1, TF, D_MODEL), lambda i, j, te, va: (m, te[i], j * va[i], 0))],
        out_specs=row,
        scratch_shapes=[pltpu.VMEM((tm, D_MODEL), F32)],
    )
    return pl.pallas_call(
        _moe_ffn_kernel,
        grid_spec=grid_spec,
        out_shape=jax.ShapeDtypeStruct((MOE_ROWS, D_MODEL), F32),
        compiler_params=_params(("arbitrary", "arbitrary"), 56),
        name="ffn_moe",
    )(tile_expert, tile_valid, xs, wg, wu, wd)


def _moe_combine_kernel(p0_ref, p1_ref, x_ref, g2_ref, wt_ref, y_ref, o_ref, buf_ref, sem):
    tm = TM_COMBINE
    step = pl.program_id(0)
    slot = step % 2

    def gather(tile, sl):
        base = tile * tm

        for r in range(tm):
            _row_copy(y_ref, p0_ref[base + r], buf_ref.at[sl, 0], r, sem.at[sl]).start(priority=r % 2)
            _row_copy(y_ref, p1_ref[base + r], buf_ref.at[sl, 1], r, sem.at[sl]).start(priority=(r + 1) % 2)

    @pl.when(step == 0)
    def _():
        gather(step, slot)

    @pl.when(step + 1 < pl.num_programs(0))
    def _():
        gather(step + 1, 1 - slot)

    for k in range(2):
        pltpu.make_async_copy(y_ref.at[pl.ds(0, tm)], buf_ref.at[slot, k], sem.at[slot]).wait()
    w = wt_ref[...]
    f = w[:, 0:1] * buf_ref[slot, 0] + w[:, 1:2] * buf_ref[slot, 1]
    o_ref[...] = x_ref[...] + g2_ref[0] * f


def _moe_combine(pos0, pos1, x, gate2, wts, y):
    tm = TM_COMBINE
    per_batch = SEQ // tm
    row = pl.BlockSpec((tm, D_MODEL), lambda i, p0, p1: (i, 0))
    grid_spec = pltpu.PrefetchScalarGridSpec(
        num_scalar_prefetch=2,
        grid=(TOKENS // tm,),
        in_specs=[row,
                  pl.BlockSpec((1, 1, D_MODEL), lambda i, p0, p1: (i // per_batch, 0, 0)),
                  pl.BlockSpec((tm, LANES), lambda i, p0, p1: (i, 0)),
                  pl.BlockSpec(memory_space=pl.ANY)],
        out_specs=row,
        scratch_shapes=[pltpu.VMEM((2, 2, tm, D_MODEL), F32), pltpu.SemaphoreType.DMA((2,))],
    )
    return pl.pallas_call(
        _moe_combine_kernel,
        grid_spec=grid_spec,
        out_shape=jax.ShapeDtypeStruct((TOKENS, D_MODEL), F32),
        compiler_params=_params(("arbitrary",), 32),
        name="moe_combine",
    )(pos0, pos1, x, gate2, wts, y)


def _prep_w_in(w_in):
    cq_ckv = w_in[..., 0:Q_RANK + KV_RANK]
    kr = w_in[..., Q_RANK + KV_RANK:Q_RANK + KV_RANK + ROPE]
    rest = w_in[..., Q_RANK + KV_RANK + ROPE:]
    x1, x2 = kr[..., :ROPE // 2], kr[..., ROPE // 2:]
    z = lambda n: jnp.zeros(kr.shape[:-1] + (n,), w_in.dtype)
    kr_a = jnp.concatenate([z(NOPE), x1, x2, z(HEAD_TILE - QK_DIM)], axis=-1)
    kr_b = jnp.concatenate([z(NOPE), -x2, x1, z(HEAD_TILE - QK_DIM)], axis=-1)
    return jnp.concatenate([cq_ckv, kr_a, kr_b, rest], axis=-1).astype(BF16)


def _prep_w_uq(w_uq):
    w = w_uq.reshape(DEPTH, Q_RANK, N_HEADS, QK_DIM)
    nope, x1, x2 = w[..., :NOPE], w[..., NOPE:NOPE + ROPE // 2], w[..., NOPE + ROPE // 2:]
    z = lambda n: jnp.zeros(w.shape[:-1] + (n,), w.dtype)
    qa = jnp.concatenate([nope, x1, x2, z(HEAD_TILE - QK_DIM)], axis=-1)
    qb = jnp.concatenate([z(NOPE), -x2, x1, z(HEAD_TILE - QK_DIM)], axis=-1)
    flat = lambda a: a.reshape(DEPTH, Q_RANK, N_HEADS * HEAD_TILE)
    return jnp.concatenate([flat(qa), flat(qb)], axis=-1).astype(BF16)


def _prep_w_ukv(w_ukv):
    w = w_ukv.reshape(DEPTH, KV_RANK, N_HEADS, NOPE + V_DIM)
    k_nope, v = w[..., :NOPE], w[..., NOPE:]
    zk = jnp.zeros(k_nope.shape[:-1] + (HEAD_TILE - NOPE,), w.dtype)
    ka = jnp.concatenate([k_nope, zk], axis=-1)
    zv = jnp.zeros_like(v)
    even = (jnp.arange(N_HEADS) % 2 == 0)[None, None, :, None]
    vp = jnp.concatenate([jnp.where(even, v, zv), jnp.where(even, zv, v)], axis=-1)
    flat = lambda a: a.reshape(DEPTH, KV_RANK, N_HEADS * HEAD_TILE)
    return jnp.concatenate([flat(ka), flat(vp)], axis=-1).astype(BF16)


def _head_gain(g, scale):
    pad = jnp.zeros((DEPTH, HEAD_TILE - QK_DIM), g.dtype)
    return (jnp.concatenate([g, pad], axis=-1) * scale).reshape(DEPTH, 1, HEAD_TILE)


def _rope_tables(positions):
    inv_freq = 1.0 / (10000.0 ** (jnp.arange(0, ROPE, 2, dtype=F32) / ROPE))
    ang = positions.astype(F32).reshape(TOKENS, 1) * inv_freq
    cos, sin = jnp.cos(ang), jnp.sin(ang)
    cos_t = jnp.concatenate([jnp.ones((TOKENS, NOPE), F32), cos, cos,
                             jnp.zeros((TOKENS, HEAD_TILE - QK_DIM), F32)], axis=-1)
    sin_t = jnp.concatenate([jnp.zeros((TOKENS, NOPE), F32), sin, sin,
                             jnp.zeros((TOKENS, HEAD_TILE - QK_DIM), F32)], axis=-1)
    return cos_t, sin_t


def _moe_plan(meta, cnt):
    e1, e2, r1, r2 = meta[:, 0], meta[:, 1], meta[:, 2], meta[:, 3]
    counts = cnt[0, :N_EXPERTS].astype(jnp.int32)
    padded = ((counts + TM_FFN - 1) // TM_FFN) * TM_FFN
    ends = jnp.cumsum(padded)
    starts = ends - padded
    experts = jnp.arange(N_EXPERTS, dtype=jnp.int32)
    start_of = lambda e: jnp.sum(jnp.where(e[:, None] == experts[None, :], starts[None, :], 0), axis=1)
    pos0 = start_of(e1) + r1
    pos1 = start_of(e2) + r2
    tile_start = jnp.arange(N_MOE_TILES, dtype=jnp.int32) * TM_FFN
    tile_expert = jnp.minimum(jnp.sum(tile_start[:, None] >= ends[None, :], axis=1), N_EXPERTS - 1)
    tile_valid = (tile_start < ends[-1]).astype(jnp.int32)
    group_end = jnp.where(experts == N_EXPERTS - 1, MOE_ROWS, ends)
    pads = (starts + counts, group_end - (starts + counts))
    return pos0.astype(jnp.int32), pos1.astype(jnp.int32), tile_expert.astype(jnp.int32), tile_valid, pads


def kernel(x, c, positions, ada_w, ada_b, norm1_g, norm2_g, w_in, q_norm_g, kv_norm_g, w_uq, w_ukv, q_head_g,
           k_head_g, w_o_mla, conv_w, conv_b, lru_gate_w, lru_gate_b, lru_a_param, w_o_lru, w_out, ffn_w_gate,
           ffn_w_up, ffn_w_down, moe_router, moe_w_gate, moe_w_up, moe_w_down):
    cos_t, sin_t = _rope_tables(positions)
    mod = _modulation(c, ada_w, ada_b).reshape(DEPTH, BATCH, 6, 1, D_MODEL)

    vec = lambda a: a.reshape(DEPTH, 1, a.shape[-1])
    w_in_p = _prep_w_in(w_in)
    wq_p = _prep_w_uq(w_uq)
    wkv_p = _prep_w_ukv(w_ukv)
    hq = _head_gain(q_head_g, 0.5 * TWO_LOG2E * QK_DIM ** -0.5)
    hk = _head_gain(k_head_g, 1.0)
    wg_p = (0.5 * lru_gate_w).transpose(0, 3, 4, 1, 2, 5).reshape(
        DEPTH, N_LRU_BLOCKS, LRU_BLOCK, 4 * LRU_BLOCK).astype(BF16)
    bg_p = (0.5 * lru_gate_b).reshape(DEPTH, 2, 2, N_LRU_BLOCKS, LRU_BLOCK).transpose(0, 3, 1, 2, 4).reshape(
        DEPTH, N_LRU_BLOCKS, 1, 4 * LRU_BLOCK)
    ap_p = lru_a_param.reshape(DEPTH, 2, N_LRU_BLOCKS, LRU_BLOCK).transpose(0, 2, 1, 3).reshape(
        DEPTH, N_LRU_BLOCKS, 1, 2 * LRU_BLOCK)
    conv_b_p = conv_b.reshape(DEPTH, 1, LRU_W)
    wol, wom, wout = w_o_lru.astype(BF16), w_o_mla.astype(BF16), w_out.astype(BF16)
    fg, fu, fd = ffn_w_gate.astype(BF16), ffn_w_up.astype(BF16), ffn_w_down.astype(BF16)
    mg, mu, md = moe_w_gate.astype(BF16), moe_w_up.astype(BF16), moe_w_down
    router_hi = moe_router.astype(BF16)
    router_lo = (moe_router - router_hi.astype(F32)).astype(BF16)
    router = jnp.pad(jnp.concatenate([router_hi, router_lo], axis=-1),
                     ((0, 0), (0, 0), (0, LANES - 2 * N_EXPERTS)))
    n1, n2, gq, gkv = vec(norm1_g), vec(norm2_g), vec(q_norm_g), vec(kv_norm_g)

    xt = x.reshape(TOKENS, D_MODEL)
    for l in range(DEPTH):
        shift1, scale1, gate1, shift2, scale2, gate2 = [mod[l, :, k] for k in range(6)]
        q, k, v, u, y_gate, gb = _in_projection(l, xt, shift1, scale1, n1, w_in_p, cos_t, sin_t, gq, gkv,
                                                wq_p, wkv_p, hq, hk)
        attn = _attention(q, k, v).reshape(TOKENS, N_HEADS * V_DIM)
        h_lru = _lru_branch(l, u.reshape(BATCH, SEQ, LRU_W), conv_w, conv_b_p, wg_p, bg_p, ap_p)
        h_lru = h_lru.reshape(TOKENS, LRU_W)
        m = l // 2
        if l % 2 == 0:
            xt, h2 = _mix(l, h_lru, y_gate, attn, gb, xt, gate1, shift2, scale2, n2, wol, wom, wout)
            xt = _dense_ffn(m, h2, xt, gate2, fg, fu, fd)
        else:
            xt, h2, meta, wts, cnt = _mix(l, h_lru, y_gate, attn, gb, xt, gate1, shift2, scale2, n2, wol, wom,
                                          wout, router=router)
            pos0, pos1, tile_expert, tile_valid, (pad_start, pad_len) = _moe_plan(meta, cnt)
            xs = _dispatch(pos0, pos1, pad_start, pad_len, h2)
            y = _moe_ffn(m, tile_expert, tile_valid, xs, mg, mu, md)
            xt = _moe_combine(pos0, pos1, xt, gate2, wts, y)
    return xt.reshape(BATCH, SEQ, D_MODEL)
```

```python
import functools

import jax
import jax.numpy as jnp
from jax import lax
from jax.experimental import pallas as pl
from jax.experimental.pallas import tpu as pltpu

F32 = jnp.float32
BF16 = jnp.bfloat16

D_MODEL = 1024
BATCH = 8
SEQ = 2048
TOKENS = BATCH * SEQ
DEPTH = 4
N_HEADS = 8
NOPE = 64
ROPE = 32
QK_DIM = NOPE + ROPE
V_DIM = 64
Q_RANK = 256
KV_RANK = 256
LRU_W = D_MODEL
N_LRU_BLOCKS = 8
LRU_BLOCK = LRU_W // N_LRU_BLOCKS
RG_LRU_C = 8.0
D_FF = 2816
N_EXPERTS = 8
N_MOE_LAYERS = DEPTH // 2
EPS = 1e-6

LANES = 128
SUBLANES = 8
HEAD_TILE = LANES

MLA_COLS = Q_RANK + KV_RANK + 2 * HEAD_TILE
IN_COLS = MLA_COLS + 2 * LRU_W + 2 * D_MODEL

TM_PROJ = 512
TQ = 512
TM_FFN = 512
TF = D_FF // 2
N_MOE_TILES = (2 * TOKENS) // TM_FFN + N_EXPERTS
MOE_ROWS = N_MOE_TILES * TM_FFN
TM_DISPATCH = 512
TM_COMBINE = 256

MIB = 1024 * 1024


def _params(sem, vmem_mib):
    return pltpu.CompilerParams(dimension_semantics=sem, vmem_limit_bytes=vmem_mib * MIB)


def _dot(a, b):
    return jnp.dot(a, b, preferred_element_type=F32)


def _split_bf16(a):
    hi = a.astype(BF16)
    lo = (a - hi.astype(F32)).astype(BF16)
    return hi, lo


def _rms(v, inv_n):
    return v * lax.rsqrt(jnp.sum(v * v, axis=-1, keepdims=True) * inv_n + EPS)


def _layer_spec(a, l):
    nd = a.ndim - 1
    return pl.BlockSpec((1,) + a.shape[1:], lambda i: (l,) + (0,) * nd)


TN_MOD = 1536


def _mod_kernel(c_ref, w_ref, b_ref, o_ref):
    c = c_ref[...]
    ch, cl = _split_bf16(c * jax.nn.sigmoid(c))
    wh, wl = _split_bf16(w_ref[0])
    o_ref[0] = _dot(ch, wh) + (_dot(ch, wl) + _dot(cl, wh)) + b_ref[0]


def _modulation(c, ada_w, ada_b):
    n = 6 * D_MODEL
    return pl.pallas_call(
        _mod_kernel,
        grid=(DEPTH, n // TN_MOD),
        in_specs=[
            pl.BlockSpec((BATCH, D_MODEL), lambda l, j: (0, 0)),
            pl.BlockSpec((1, D_MODEL, TN_MOD), lambda l, j: (l, 0, j)),
            pl.BlockSpec((1, 1, TN_MOD), lambda l, j: (l, 0, j)),
        ],
        out_specs=pl.BlockSpec((1, BATCH, TN_MOD), lambda l, j: (l, 0, j)),
        out_shape=jax.ShapeDtypeStruct((DEPTH, BATCH, n), F32),
        compiler_params=_params(("arbitrary", "arbitrary"), 40),
        name="adaln_mod",
    )(c, ada_w, ada_b.reshape(DEPTH, 1, n))


def _batch_vec_spec(tm):
    per_batch = SEQ // tm
    return pl.BlockSpec((1, 1, D_MODEL), lambda i: (i // per_batch, 0, 0))


def _resident_spec(a, l):
    nd = a.ndim - 1
    return pl.BlockSpec((1,) + a.shape[1:], lambda i: (l,) + (0,) * nd, pipeline_mode=pl.Buffered(1))


def _inproj_kernel(x_ref, sh_ref, sc_ref, g_ref, w_ref, cos_ref, sin_ref, gq_ref, gkv_ref, wq_ref, wkv_ref,
                   hq_ref, hk_ref, q_ref, k_ref, v_ref, u_ref, y_ref, gb_ref):
    x = x_ref[...]
    h = _rms(x, 1.0 / D_MODEL) * g_ref[0]
    h = (h * (1.0 + sc_ref[0]) + sh_ref[0]).astype(BF16)
    c0, c1, c2 = MLA_COLS, MLA_COLS + LRU_W, MLA_COLS + 2 * LRU_W
    m = _dot(h, w_ref[0, :, 0:c0])
    u_ref[...] = _dot(h, w_ref[0, :, c0:c1])
    y_ref[...] = _dot(h, w_ref[0, :, c1:c2]).astype(BF16)
    gb_ref[...] = _dot(h, w_ref[0, :, c2:IN_COLS]).astype(BF16)

    cos = cos_ref[...]
    sin = sin_ref[...]
    cq = m[:, 0:Q_RANK]
    ckv = m[:, Q_RANK:Q_RANK + KV_RANK]
    kr_a = m[:, Q_RANK + KV_RANK:Q_RANK + KV_RANK + HEAD_TILE]
    kr_b = m[:, Q_RANK + KV_RANK + HEAD_TILE:MLA_COLS]
    cqn = (_rms(cq, 1.0 / Q_RANK) * gq_ref[0]).astype(BF16)
    ckvn = (_rms(ckv, 1.0 / KV_RANK) * gkv_ref[0]).astype(BF16)
    qq = _dot(cqn, wq_ref[0])
    kv = _dot(ckvn, wkv_ref[0])
    k_pe = kr_a * cos + kr_b * sin
    nq = N_HEADS * HEAD_TILE
    lane = lax.broadcasted_iota(jnp.int32, (TM_PROJ, HEAD_TILE), 1)
    for hd in range(N_HEADS):
        lo, hi = hd * HEAD_TILE, (hd + 1) * HEAD_TILE
        qh = qq[:, lo:hi] * cos + qq[:, nq + lo:nq + hi] * sin
        q_ref[0, hd] = (_rms(qh, 1.0 / QK_DIM) * hq_ref[0]).astype(BF16)
        kh = kv[:, lo:hi] + k_pe
        k_ref[0, hd] = (_rms(kh, 1.0 / QK_DIM) * hk_ref[0]).astype(BF16)
        v_ref[0, hd] = jnp.where(lane == V_DIM * (1 - hd % 2), 1.0, kv[:, nq + lo:nq + hi]).astype(BF16)


def _in_projection(l, x, shift, scale, gain, w, cos_t, sin_t, gq, gkv, wq, wkv, hq, hk):
    tm = TM_PROJ
    per_batch = SEQ // tm
    row = lambda n: pl.BlockSpec((tm, n), lambda i: (i, 0))
    head = pl.BlockSpec((1, N_HEADS, tm, HEAD_TILE), lambda i: (i // per_batch, 0, i % per_batch, 0))
    head_shape = jax.ShapeDtypeStruct((BATCH, N_HEADS, SEQ, HEAD_TILE), BF16)
    params = [gq, gkv, wq, wkv, hq, hk]
    return pl.pallas_call(
        _inproj_kernel,
        grid=(TOKENS // tm,),
        in_specs=[row(D_MODEL), _batch_vec_spec(tm), _batch_vec_spec(tm), _layer_spec(gain, l),
                  _resident_spec(w, l), row(HEAD_TILE), row(HEAD_TILE)] + [_resident_spec(p, l) for p in params],
        out_specs=[head, head, head, row(LRU_W), row(LRU_W), row(2 * D_MODEL)],
        out_shape=[
            head_shape, head_shape, head_shape,
            jax.ShapeDtypeStruct((TOKENS, LRU_W), F32),
            jax.ShapeDtypeStruct((TOKENS, LRU_W), BF16),
            jax.ShapeDtypeStruct((TOKENS, 2 * D_MODEL), BF16),
        ],
        compiler_params=_params(("arbitrary",), 56),
        name="in_proj",
    )(x, shift, scale, gain, w, cos_t, sin_t, *params)


def _attn_kernel(q_ref, k_ref, v_ref, o_ref):
    lane = lax.broadcasted_iota(jnp.int32, (TQ, LANES), 1)
    for pair in range(N_HEADS // 2):
        acc = None
        for j in range(2):
            h = 2 * pair + j
            s = lax.dot_general(q_ref[0, h], k_ref[0, h], (((1,), (1,)), ((), ())),
                                preferred_element_type=F32)
            p = jnp.exp2(s - jnp.max(s, axis=-1, keepdims=True)).astype(BF16)
            o = _dot(p, v_ref[0, h])
            ones_lane = V_DIM * (1 - j)
            inv = 1.0 / o[:, ones_lane:ones_lane + 1]
            mine = (lane < V_DIM) if j == 0 else (lane >= V_DIM)
            o = jnp.where(mine, o * inv, 0.0)
            acc = o if acc is None else acc + o
        o_ref[0, :, pair * LANES:(pair + 1) * LANES] = acc.astype(BF16)


def _attention(q, k, v):
    nq = SEQ // TQ
    kv_spec = pl.BlockSpec((1, N_HEADS, SEQ, HEAD_TILE), lambda b, i: (b, 0, 0, 0))
    return pl.pallas_call(
        _attn_kernel,
        grid=(BATCH, nq),
        in_specs=[pl.BlockSpec((1, N_HEADS, TQ, HEAD_TILE), lambda b, i: (b, 0, i, 0)), kv_spec, kv_spec],
        out_specs=pl.BlockSpec((1, TQ, N_HEADS * V_DIM), lambda b, i: (b, i, 0)),
        out_shape=jax.ShapeDtypeStruct((BATCH, SEQ, N_HEADS * V_DIM), BF16),
        compiler_params=_params(("arbitrary", "arbitrary"), 48),
        name="attention",
    )(q, k, v)


CHUNK = SEQ // SUBLANES
PITCH = CHUNK + 4
CONV_BEFORE, CONV_AFTER = 2, 1
TWO_LOG2E = 2.0 * 1.4426950408889634


def _shift_rows(v, row, down):
    if down:
        return jnp.where(row == 0, 0.0, pltpu.roll(v, 1, 0))
    return jnp.where(row == SUBLANES - 1, 0.0, pltpu.roll(v, SUBLANES - 1, 0))


def _scan_sublanes(a, b, row, reverse):
    for s in (1, 2, 4):
        shift = SUBLANES - s if reverse else s
        a_s = pltpu.roll(a, shift, 0)
        b_s = pltpu.roll(b, shift, 0)
        keep = (row < SUBLANES - s) if reverse else (row >= s)
        b = jnp.where(keep, a * b_s + b, b)
        a = jnp.where(keep, a * a_s, a)
    return b


def _lru_kernel(u_hbm, cw_ref, cb_ref, wg_ref, bg_ref, ap_ref, h_hbm,
                uin_ref, hout_ref, up_ref, uc_ref, g_ref, a_ref, b_ref, h_ref, p_ref, in_sem, out_sem):
    n_steps = pl.num_programs(0) * N_LRU_BLOCKS
    step = pl.program_id(0) * N_LRU_BLOCKS + pl.program_id(1)
    slot = step % 2

    def chunk_copies(s, sl, inbound):
        bb = s // N_LRU_BLOCKS
        col = pl.multiple_of((s % N_LRU_BLOCKS) * LRU_BLOCK, LRU_BLOCK)
        out = []
        for j in range(SUBLANES):
            hbm = (u_hbm if inbound else h_hbm).at[bb, pl.ds(j * CHUNK, CHUNK), pl.ds(col, LRU_BLOCK)]
            if inbound:
                out.append(pltpu.make_async_copy(hbm, uin_ref.at[sl, pl.ds(j * PITCH, CHUNK), :], in_sem.at[sl]))
            else:
                out.append(pltpu.make_async_copy(hout_ref.at[sl, pl.ds(j * PITCH, CHUNK), :], hbm, out_sem.at[sl]))
        return out

    @pl.when(step == 0)
    def _():
        for cp in chunk_copies(step, slot, True):
            cp.start()

    @pl.when(step + 1 < n_steps)
    def _():
        for cp in chunk_copies(step + 1, 1 - slot, True):
            cp.start()

    for cp in chunk_copies(step, slot, True):
        cp.wait()

    row = lax.broadcasted_iota(jnp.int32, (SUBLANES, LRU_BLOCK), 0)

    def tile_rows(k):
        return slice(k * SUBLANES, (k + 1) * SUBLANES)

    for k in range(CHUNK):
        up_ref[tile_rows(k + CONV_BEFORE), :] = uin_ref[slot, pl.ds(k, SUBLANES, stride=PITCH), :]
    for i in range(CONV_BEFORE):
        src = up_ref[(CHUNK + i) * SUBLANES:(CHUNK + i + 1) * SUBLANES, :]
        up_ref[i * SUBLANES:(i + 1) * SUBLANES, :] = _shift_rows(src, row, True)
    for i in range(CONV_AFTER):
        src = up_ref[(CONV_BEFORE + i) * SUBLANES:(CONV_BEFORE + i + 1) * SUBLANES, :]
        dst = (CONV_BEFORE + CHUNK + i) * SUBLANES
        up_ref[dst:dst + SUBLANES, :] = _shift_rows(src, row, False)

    uc = cb_ref[0]
    for tap in range(CONV_BEFORE + CONV_AFTER + 1):
        uc = uc + up_ref[tap * SUBLANES:tap * SUBLANES + SEQ, :] * cw_ref[0, tap:tap + 1, :]
    uc_ref[...] = uc
    g_ref[...] = _dot(uc.astype(BF16), wg_ref[0, 0]) + bg_ref[0, 0]

    sp = jax.nn.softplus(ap_ref[0, 0])
    u_half = 0.5 * uc_ref[...]
    for d in range(2):
        quarter_coef = (-0.25 * RG_LRU_C) * sp[:, d * LRU_BLOCK:(d + 1) * LRU_BLOCK]
        g_r = g_ref[:, 2 * d * LRU_BLOCK:(2 * d + 1) * LRU_BLOCK]
        g_i = g_ref[:, (2 * d + 1) * LRU_BLOCK:(2 * d + 2) * LRU_BLOCK]
        x = quarter_coef + quarter_coef * jnp.tanh(g_r)
        a = jnp.exp2(TWO_LOG2E * x)
        s = -jnp.tanh(x)
        mult = (1.0 + a) * jnp.where(s > 0.0, s * lax.rsqrt(s), 0.0)
        gated = u_half + u_half * jnp.tanh(g_i)
        a_ref[d] = a
        b_ref[d] = mult * gated
        first = slice(0, SUBLANES) if d == 0 else slice(SEQ - SUBLANES, SEQ)
        first_row = 0 if d == 0 else SUBLANES - 1
        g_first = u_half[first, :] + u_half[first, :] * jnp.tanh(g_i[first, :])
        b_ref[d, first, :] = jnp.where(row == first_row, g_first, b_ref[d, first, :])

    zeros = jnp.zeros((SUBLANES, LRU_BLOCK), F32)
    ones = jnp.ones((SUBLANES, LRU_BLOCK), F32)
    state = [(zeros, ones), (zeros, ones)]
    for m in range(CHUNK // 2):
        for d in range(2):
            k0, k1 = (2 * m, 2 * m + 1) if d == 0 else (CHUNK - 1 - 2 * m, CHUNK - 2 - 2 * m)
            r0, r1 = tile_rows(k0), tile_rows(k1)
            a0, b0 = a_ref[d, r0, :], b_ref[d, r0, :]
            a1, b1 = a_ref[d, r1, :], b_ref[d, r1, :]
            a01 = a1 * a0
            b01 = a1 * b0 + b1
            h, p = state[d]
            h_ref[d, r0, :] = a0 * h + b0
            p_ref[d, r0, :] = a0 * p
            h = a01 * h + b01
            p = a01 * p
            h_ref[d, r1, :] = h
            p_ref[d, r1, :] = p
            state[d] = (h, p)
    (hf, af), (hb, ab) = state
    carry_f = _shift_rows(_scan_sublanes(af, hf, row, False), row, True)
    carry_b = _shift_rows(_scan_sublanes(ab, hb, row, True), row, False)

    @pl.when(step >= 2)
    def _():
        for cp in chunk_copies(step, slot, False):
            cp.wait()

    for k in range(CHUNK):
        r = tile_rows(k)
        h = (h_ref[0, r, :] + p_ref[0, r, :] * carry_f) + (h_ref[1, r, :] + p_ref[1, r, :] * carry_b)
        hout_ref[slot, pl.ds(k, SUBLANES, stride=PITCH), :] = h
    for cp in chunk_copies(step, slot, False):
        cp.start()

    @pl.when(step == n_steps - 1)
    def _():
        for cp in chunk_copies(step, 1 - slot, False) + chunk_copies(step, slot, False):
            cp.wait()


def _lru_branch(l, u, conv_w, conv_b, wg, bg, a_param):
    batch = u.shape[0]
    vm = lambda *shape: pltpu.VMEM(shape, F32)
    stage = (2, SUBLANES * PITCH, LRU_BLOCK)
    return pl.pallas_call(
        _lru_kernel,
        grid=(batch, N_LRU_BLOCKS),
        in_specs=[
            pl.BlockSpec(memory_space=pl.ANY),
            pl.BlockSpec((1, 4, LRU_BLOCK), lambda b, n: (l, 0, n)),
            pl.BlockSpec((1, 1, LRU_BLOCK), lambda b, n: (l, 0, n)),
            pl.BlockSpec((1, 1, LRU_BLOCK, 4 * LRU_BLOCK), lambda b, n: (l, n, 0, 0)),
            pl.BlockSpec((1, 1, 1, 4 * LRU_BLOCK), lambda b, n: (l, n, 0, 0)),
            pl.BlockSpec((1, 1, 1, 2 * LRU_BLOCK), lambda b, n: (l, n, 0, 0)),
        ],
        out_specs=pl.BlockSpec(memory_space=pl.ANY),
        out_shape=jax.ShapeDtypeStruct((batch, SEQ, LRU_W), F32),
        scratch_shapes=[vm(*stage), vm(*stage),
                        vm(SEQ + (CONV_BEFORE + CONV_AFTER) * SUBLANES, LRU_BLOCK), vm(SEQ, LRU_BLOCK),
                        vm(SEQ, 4 * LRU_BLOCK)] + [vm(2, SEQ, LRU_BLOCK)] * 4 + [
                        pltpu.SemaphoreType.DMA((2,)), pltpu.SemaphoreType.DMA((2,))],
        compiler_params=_params(("arbitrary", "arbitrary"), 40),
        name="rg_lru",
    )(u, conv_w, conv_b, wg, bg, a_param)


def _gelu_tanh(x):
    return 0.5 * x * (1.0 + jnp.tanh(0.7978845608028654 * (x + 0.044715 * x * x * x)))


def _route(lg, meta_ref, wt_ref, cnt_ref, carry_ref):
    tm = lg.shape[0]

    @pl.when(pl.program_id(0) == 0)
    def _():
        carry_ref[...] = jnp.zeros_like(carry_ref)

    lane = lax.broadcasted_iota(jnp.int32, (tm, LANES), 1)
    lane_f = lane.astype(F32)
    neg = jnp.float32(-jnp.inf)
    lg = jnp.where(lane < N_EXPERTS, lg, neg)
    m1 = jnp.max(lg, axis=-1, keepdims=True)
    i1 = jnp.min(jnp.where(lg == m1, lane_f, float(LANES)), axis=-1, keepdims=True).astype(jnp.int32)
    lg2 = jnp.where(lane == i1, neg, lg)
    m2 = jnp.max(lg2, axis=-1, keepdims=True)
    i2 = jnp.min(jnp.where(lg2 == m2, lane_f, float(LANES)), axis=-1, keepdims=True).astype(jnp.int32)
    e = jnp.exp(m2 - m1)
    w1 = 1.0 / (1.0 + e)
    w2 = e * w1
    sel = jnp.where(lane == i1, 1.0, jnp.where(lane == i2, 1.0, 0.0))
    r_i = lax.broadcasted_iota(jnp.int32, (tm, tm), 0)
    c_i = lax.broadcasted_iota(jnp.int32, (tm, tm), 1)
    tri = jnp.where(r_i > c_i, 1.0, 0.0).astype(BF16)
    cum = _dot(tri, sel.astype(BF16)) + carry_ref[0:1, :]
    r1 = jnp.sum(jnp.where(lane == i1, cum, 0.0), axis=-1, keepdims=True).astype(jnp.int32)
    r2 = jnp.sum(jnp.where(lane == i2, cum, 0.0), axis=-1, keepdims=True).astype(jnp.int32)
    total = carry_ref[0:1, :] + jnp.sum(sel, axis=0, keepdims=True)
    carry_ref[...] = jnp.broadcast_to(total, carry_ref.shape)
    cnt_ref[...] = jnp.broadcast_to(total, cnt_ref.shape)
    meta_ref[...] = jnp.where(lane == 0, i1, jnp.where(lane == 1, i2, jnp.where(lane == 2, r1,
                              jnp.where(lane == 3, r2, 0))))
    wt_ref[...] = jnp.where(lane == 0, w1, jnp.where(lane == 1, w2, 0.0))


def _mix_kernel(*refs, moe):
    if moe:
        (h_ref, y_ref, at_ref, gb_ref, x_ref, g1_ref, sh_ref, sc_ref, n2_ref, wol_ref, wom_ref, wout_ref,
         wr_ref, xo_ref, h2_ref, meta_ref, wt_ref, cnt_ref, carry_ref) = refs
    else:
        (h_ref, y_ref, at_ref, gb_ref, x_ref, g1_ref, sh_ref, sc_ref, n2_ref, wol_ref, wom_ref, wout_ref,
         xo_ref, h2_ref) = refs
    hg = (h_ref[...] * _gelu_tanh(y_ref[...].astype(F32))).astype(BF16)
    y_lru = _dot(hg, wol_ref[0])
    y_mla = _dot(at_ref[...], wom_ref[0])
    g_lru = jax.nn.sigmoid(gb_ref[:, 0:D_MODEL].astype(F32))
    g_mla = jax.nn.sigmoid(gb_ref[:, D_MODEL:2 * D_MODEL].astype(F32))
    z = (g_lru * y_lru + g_mla * y_mla).astype(BF16)
    xn = x_ref[...] + g1_ref[0] * _dot(z, wout_ref[0])
    xo_ref[...] = xn
    h2 = _rms(xn, 1.0 / D_MODEL) * n2_ref[0]
    h2 = h2 * (1.0 + sc_ref[0]) + sh_ref[0]
    if moe:
        h2_ref[...] = h2
        hh, hl = _split_bf16(h2)
        parts = _dot(hh, wr_ref[0]) + _dot(hl, wr_ref[0])
        _route(parts + pltpu.roll(parts, LANES - N_EXPERTS, 1), meta_ref, wt_ref, cnt_ref, carry_ref)
    else:
        h2_ref[...] = h2.astype(BF16)


def _mix(l, h_lru, y_gate, attn, gb, x, gate1, shift2, scale2, n2, wol, wom, wout, router=None):
    tm = TM_PROJ
    moe = router is not None
    row = lambda n: pl.BlockSpec((tm, n), lambda i: (i, 0))
    vec = _batch_vec_spec(tm)
    ins = [h_lru, y_gate, attn, gb, x, gate1, shift2, scale2, n2, wol, wom, wout]
    in_specs = [row(LRU_W), row(LRU_W), row(N_HEADS * V_DIM), row(2 * D_MODEL), row(D_MODEL), vec, vec, vec,
                _layer_spec(n2, l), _layer_spec(wol, l), _layer_spec(wom, l), _layer_spec(wout, l)]
    out_specs = [row(D_MODEL), row(D_MODEL)]
    out_shape = [jax.ShapeDtypeStruct((TOKENS, D_MODEL), F32),
                 jax.ShapeDtypeStruct((TOKENS, D_MODEL), F32 if moe else BF16)]
    scratch = []
    if moe:
        ins.append(router)
        in_specs.append(_layer_spec(router, l // 2))
        out_specs += [row(LANES), row(LANES), pl.BlockSpec((SUBLANES, LANES), lambda i: (0, 0))]
        out_shape += [jax.ShapeDtypeStruct((TOKENS, LANES), jnp.int32),
                      jax.ShapeDtypeStruct((TOKENS, LANES), F32),
                      jax.ShapeDtypeStruct((SUBLANES, LANES), F32)]
        scratch.append(pltpu.VMEM((SUBLANES, LANES), F32))
    return pl.pallas_call(
        functools.partial(_mix_kernel, moe=moe),
        grid=(TOKENS // tm,),
        in_specs=in_specs,
        out_specs=out_specs,
        out_shape=out_shape,
        scratch_shapes=scratch,
        compiler_params=_params(("arbitrary",), 48),
        name="mix_moe" if moe else "mix_dense",
    )(*ins)


def _swiglu_accumulate(h, wg, wu, wd, acc_ref, j, finish):
    a = _dot(h, wg)
    b = _dot(h, wu)
    part = _dot((a * jax.nn.sigmoid(a) * b).astype(BF16), wd)
    last = pl.num_programs(1) - 1

    @pl.when(j == 0)
    def _():
        acc_ref[...] = part

    @pl.when(jnp.logical_and(j > 0, j < last))
    def _():
        acc_ref[...] += part

    @pl.when(j == last)
    def _():
        finish(acc_ref[...] + part)


def _ffn_kernel(h_ref, x_ref, g2_ref, wg_ref, wu_ref, wd_ref, o_ref, acc_ref):
    def finish(total):
        o_ref[...] = x_ref[...] + g2_ref[0] * total

    _swiglu_accumulate(h_ref[...], wg_ref[0], wu_ref[0], wd_ref[0], acc_ref, pl.program_id(1), finish)


def _dense_ffn(m, h2, x, gate2, wg, wu, wd):
    tm = TM_FFN
    per_batch = SEQ // tm
    row = pl.BlockSpec((tm, D_MODEL), lambda i, j: (i, 0))
    return pl.pallas_call(
        _ffn_kernel,
        grid=(TOKENS // tm, D_FF // TF),
        in_specs=[row, row, pl.BlockSpec((1, 1, D_MODEL), lambda i, j: (i // per_batch, 0, 0)),
                  pl.BlockSpec((1, D_MODEL, TF), lambda i, j: (m, 0, j)),
                  pl.BlockSpec((1, D_MODEL, TF), lambda i, j: (m, 0, j)),
                  pl.BlockSpec((1, TF, D_MODEL), lambda i, j: (m, j, 0))],
        out_specs=row,
        out_shape=jax.ShapeDtypeStruct((TOKENS, D_MODEL), F32),
        scratch_shapes=[pltpu.VMEM((tm, D_MODEL), F32)],
        compiler_params=_params(("arbitrary", "arbitrary"), 56),
        name="ffn_dense",
    )(h2, x, gate2, wg, wu, wd)


def _row_copy(src_ref, src_row, dst_ref, dst_row, sem):
    return pltpu.make_async_copy(src_ref.at[pl.ds(src_row, 1)], dst_ref.at[pl.ds(dst_row, 1)], sem)


def _dispatch_kernel(p0_ref, p1_ref, pad_start_ref, pad_len_ref, h_hbm, xs_ref, buf_ref, zero_ref,
                     load_sem, row_sem, pad_sem):
    tm = TM_DISPATCH
    step = pl.program_id(0)
    n_steps = pl.num_programs(0)
    slot = step % 2
    base = step * tm

    def load(tile, sl):
        return pltpu.make_async_copy(h_hbm.at[pl.ds(tile * tm, tm)], buf_ref.at[sl], load_sem.at[sl])

    def drain_rows():
        for _ in range(2):
            pltpu.make_async_copy(buf_ref.at[0], xs_ref.at[pl.ds(0, tm)], row_sem).wait()

    def pad_copies(act):
        for e in range(N_EXPERTS):
            start = pad_start_ref[e]
            length = pad_len_ref[e]
            head = jnp.minimum((0 - start) & (SUBLANES - 1), length)

            def one_row(i, c):
                act(_row_copy(zero_ref, 0, xs_ref, start + i, pad_sem))
                return c

            def one_block(i, c):
                first = pl.multiple_of(start + head + i * SUBLANES, SUBLANES)
                act(pltpu.make_async_copy(zero_ref, xs_ref.at[pl.ds(first, SUBLANES)], pad_sem))
                return c

            lax.fori_loop(0, head, one_row, 0)
            lax.fori_loop(0, (length - head) // SUBLANES, one_block, 0)

    @pl.when(step == 0)
    def _():
        load(step, slot).start()
        zero_ref[...] = jnp.zeros_like(zero_ref)
        pad_copies(lambda cp: cp.start())

    @pl.when(step > 0)
    def _():
        drain_rows()

    @pl.when(step + 1 < n_steps)
    def _():
        load(step + 1, 1 - slot).start()

    load(step, slot).wait()
    src = buf_ref.at[slot]

    for r in range(tm):
        _row_copy(src, r, xs_ref, p0_ref[base + r], row_sem).start(priority=r % 2)
        _row_copy(src, r, xs_ref, p1_ref[base + r], row_sem).start(priority=(r + 1) % 2)

    @pl.when(step == n_steps - 1)
    def _():
        drain_rows()
        pad_copies(lambda cp: cp.wait())


def _dispatch(pos0, pos1, pad_start, pad_len, h2):
    tm = TM_DISPATCH
    grid_spec = pltpu.PrefetchScalarGridSpec(
        num_scalar_prefetch=4,
        grid=(TOKENS // tm,),
        in_specs=[pl.BlockSpec(memory_space=pl.ANY)],
        out_specs=pl.BlockSpec(memory_space=pl.ANY),
        scratch_shapes=[pltpu.VMEM((2, tm, D_MODEL), F32), pltpu.VMEM((SUBLANES, D_MODEL), F32),
                        pltpu.SemaphoreType.DMA((2,)), pltpu.SemaphoreType.DMA(()), pltpu.SemaphoreType.DMA(())],
    )
    return pl.pallas_call(
        _dispatch_kernel,
        grid_spec=grid_spec,
        out_shape=jax.ShapeDtypeStruct((MOE_ROWS, D_MODEL), F32),
        compiler_params=_params(("arbitrary",), 32),
        name="moe_dispatch",
    )(pos0, pos1, pad_start, pad_len, h2)


def _moe_ffn_kernel(te_ref, valid_ref, x_ref, wg_ref, wu_ref, wd_ref, o_ref, acc_ref):
    i = pl.program_id(0)
    j = pl.program_id(1)
    last = j == pl.num_programs(1) - 1

    @pl.when(valid_ref[i] == 1)
    def _():
        def finish(total):
            o_ref[...] = total

        _swiglu_accumulate(x_ref[...].astype(BF16), wg_ref[0, 0], wu_ref[0, 0], wd_ref[0, 0].astype(BF16),
                           acc_ref, j, finish)

    @pl.when(jnp.logical_and(valid_ref[i] == 0, last))
    def _():
        o_ref[...] = jnp.zeros_like(o_ref)


def _moe_ffn(m, tile_expert, tile_valid, xs, wg, wu, wd):
    tm = TM_FFN
    row = pl.BlockSpec((tm, D_MODEL), lambda i, j, te, va: (i, 0))
    grid_spec = pltpu.PrefetchScalarGridSpec(
        num_scalar_prefetch=2,
        grid=(N_MOE_TILES, D_FF // TF),
        in_specs=[row,
                  pl.BlockSpec((1, 1, D_MODEL, TF), lambda i, j, te, va: (m, te[i], 0, j * va[i])),
                  pl.BlockSpec((1, 1, D_MODEL, TF), lambda i, j, te, va: (m, te[i], 0, j * va[i])),
                  pl.BlockSpec((1, 1, TF, D_MODEL), lambda i, j, te, va: (m, te[i], j * va[i], 0))],
        out_specs=row,
        scratch_shapes=[pltpu.VMEM((tm, D_MODEL), F32)],
    )
    return pl.pallas_call(
        _moe_ffn_kernel,
        grid_spec=grid_spec,
        out_shape=jax.ShapeDtypeStruct((MOE_ROWS, D_MODEL), F32),
        compiler_params=_params(("arbitrary", "arbitrary"), 56),
        name="ffn_moe",
    )(tile_expert, tile_valid, xs, wg, wu, wd)


def _moe_combine_kernel(p0_ref, p1_ref, x_ref, g2_ref, wt_ref, y_ref, o_ref, buf_ref, sem):
    tm = TM_COMBINE
    step = pl.program_id(0)
    slot = step % 2

    def gather(tile, sl):
        base = tile * tm

        for r in range(tm):
            _row_copy(y_ref, p0_ref[base + r], buf_ref.at[sl, 0], r, sem.at[sl]).start(priority=r % 2)
            _row_copy(y_ref, p1_ref[base + r], buf_ref.at[sl, 1], r, sem.at[sl]).start(priority=(r + 1) % 2)

    @pl.when(step == 0)
    def _():
        gather(step, slot)

    @pl.when(step + 1 < pl.num_programs(0))
    def _():
        gather(step + 1, 1 - slot)

    for k in range(2):
        pltpu.make_async_copy(y_ref.at[pl.ds(0, tm)], buf_ref.at[slot, k], sem.at[slot]).wait()
    w = wt_ref[...]
    f = w[:, 0:1] * buf_ref[slot, 0] + w[:, 1:2] * buf_ref[slot, 1]
    o_ref[...] = x_ref[...] + g2_ref[0] * f


def _moe_combine(pos0, pos1, x, gate2, wts, y):
    tm = TM_COMBINE
    per_batch = SEQ // tm
    row = pl.BlockSpec((tm, D_MODEL), lambda i, p0, p1: (i, 0))
    grid_spec = pltpu.PrefetchScalarGridSpec(
        num_scalar_prefetch=2,
        grid=(TOKENS // tm,),
        in_specs=[row,
                  pl.BlockSpec((1, 1, D_MODEL), lambda i, p0, p1: (i // per_batch, 0, 0)),
                  pl.BlockSpec((tm, LANES), lambda i, p0, p1: (i, 0)),
                  pl.BlockSpec(memory_space=pl.ANY)],
        out_specs=row,
        scratch_shapes=[pltpu.VMEM((2, 2, tm, D_MODEL), F32), pltpu.SemaphoreType.DMA((2,))],
    )
    return pl.pallas_call(
        _moe_combine_kernel,
        grid_spec=grid_spec,
        out_shape=jax.ShapeDtypeStruct((TOKENS, D_MODEL), F32),
        compiler_params=_params(("arbitrary",), 32),
        name="moe_combine",
    )(pos0, pos1, x, gate2, wts, y)


def _prep_w_in(w_in):
    cq_ckv = w_in[..., 0:Q_RANK + KV_RANK]
    kr = w_in[..., Q_RANK + KV_RANK:Q_RANK + KV_RANK + ROPE]
    rest = w_in[..., Q_RANK + KV_RANK + ROPE:]
    x1, x2 = kr[..., :ROPE // 2], kr[..., ROPE // 2:]
    z = lambda n: jnp.zeros(kr.shape[:-1] + (n,), w_in.dtype)
    kr_a = jnp.concatenate([z(NOPE), x1, x2, z(HEAD_TILE - QK_DIM)], axis=-1)
    kr_b = jnp.concatenate([z(NOPE), -x2, x1, z(HEAD_TILE - QK_DIM)], axis=-1)
    return jnp.concatenate([cq_ckv, kr_a, kr_b, rest], axis=-1).astype(BF16)


def _prep_w_uq(w_uq):
    w = w_uq.reshape(DEPTH, Q_RANK, N_HEADS, QK_DIM)
    nope, x1, x2 = w[..., :NOPE], w[..., NOPE:NOPE + ROPE // 2], w[..., NOPE + ROPE // 2:]
    z = lambda n: jnp.zeros(w.shape[:-1] + (n,), w.dtype)
    qa = jnp.concatenate([nope, x1, x2, z(HEAD_TILE - QK_DIM)], axis=-1)
    qb = jnp.concatenate([z(NOPE), -x2, x1, z(HEAD_TILE - QK_DIM)], axis=-1)
    flat = lambda a: a.reshape(DEPTH, Q_RANK, N_HEADS * HEAD_TILE)
    return jnp.concatenate([flat(qa), flat(qb)], axis=-1).astype(BF16)


def _prep_w_ukv(w_ukv):
    w = w_ukv.reshape(DEPTH, KV_RANK, N_HEADS, NOPE + V_DIM)
    k_nope, v = w[..., :NOPE], w[..., NOPE:]
    zk = jnp.zeros(k_nope.shape[:-1] + (HEAD_TILE - NOPE,), w.dtype)
    ka = jnp.concatenate([k_nope, zk], axis=-1)
    zv = jnp.zeros_like(v)
    even = (jnp.arange(N_HEADS) % 2 == 0)[None, None, :, None]
    vp = jnp.concatenate([jnp.where(even, v, zv), jnp.where(even, zv, v)], axis=-1)
    flat = lambda a: a.reshape(DEPTH, KV_RANK, N_HEADS * HEAD_TILE)
    return jnp.concatenate([flat(ka), flat(vp)], axis=-1).astype(BF16)


def _head_gain(g, scale):
    pad = jnp.zeros((DEPTH, HEAD_TILE - QK_DIM), g.dtype)
    return (jnp.concatenate([g, pad], axis=-1) * scale).reshape(DEPTH, 1, HEAD_TILE)


def _rope_tables(positions):
    inv_freq = 1.0 / (10000.0 ** (jnp.arange(0, ROPE, 2, dtype=F32) / ROPE))
    ang = positions.astype(F32).reshape(TOKENS, 1) * inv_freq
    cos, sin = jnp.cos(ang), jnp.sin(ang)
    cos_t = jnp.concatenate([jnp.ones((TOKENS, NOPE), F32), cos, cos,
                             jnp.zeros((TOKENS, HEAD_TILE - QK_DIM), F32)], axis=-1)
    sin_t = jnp.concatenate([jnp.zeros((TOKENS, NOPE), F32), sin, sin,
                             jnp.zeros((TOKENS, HEAD_TILE - QK_DIM), F32)], axis=-1)
    return cos_t, sin_t


def _moe_plan(meta, cnt):
    e1, e2, r1, r2 = meta[:, 0], meta[:, 1], meta[:, 2], meta[:, 3]
    counts = cnt[0, :N_EXPERTS].astype(jnp.int32)
    padded = ((counts + TM_FFN - 1) // TM_FFN) * TM_FFN
    ends = jnp.cumsum(padded)
    starts = ends - padded
    experts = jnp.arange(N_EXPERTS, dtype=jnp.int32)
    start_of = lambda e: jnp.sum(jnp.where(e[:, None] == experts[None, :], starts[None, :], 0), axis=1)
    pos0 = start_of(e1) + r1
    pos1 = start_of(e2) + r2
    tile_start = jnp.arange(N_MOE_TILES, dtype=jnp.int32) * TM_FFN
    tile_expert = jnp.minimum(jnp.sum(tile_start[:, None] >= ends[None, :], axis=1), N_EXPERTS - 1)
    tile_valid = (tile_start < ends[-1]).astype(jnp.int32)
    group_end = jnp.where(experts == N_EXPERTS - 1, MOE_ROWS, ends)
    pads = (starts + counts, group_end - (starts + counts))
    return pos0.astype(jnp.int32), pos1.astype(jnp.int32), tile_expert.astype(jnp.int32), tile_valid, pads


def kernel(x, c, positions, ada_w, ada_b, norm1_g, norm2_g, w_in, q_norm_g, kv_norm_g, w_uq, w_ukv, q_head_g,
           k_head_g, w_o_mla, conv_w, conv_b, lru_gate_w, lru_gate_b, lru_a_param, w_o_lru, w_out, ffn_w_gate,
           ffn_w_up, ffn_w_down, moe_router, moe_w_gate, moe_w_up, moe_w_down):
    cos_t, sin_t = _rope_tables(positions)
    mod = _modulation(c, ada_w, ada_b).reshape(DEPTH, BATCH, 6, 1, D_MODEL)

    vec = lambda a: a.reshape(DEPTH, 1, a.shape[-1])
    w_in_p = _prep_w_in(w_in)
    wq_p = _prep_w_uq(w_uq)
    wkv_p = _prep_w_ukv(w_ukv)
    hq = _head_gain(q_head_g, 0.5 * TWO_LOG2E * QK_DIM ** -0.5)
    hk = _head_gain(k_head_g, 1.0)
    wg_p = (0.5 * lru_gate_w).transpose(0, 3, 4, 1, 2, 5).reshape(
        DEPTH, N_LRU_BLOCKS, LRU_BLOCK, 4 * LRU_BLOCK).astype(BF16)
    bg_p = (0.5 * lru_gate_b).reshape(DEPTH, 2, 2, N_LRU_BLOCKS, LRU_BLOCK).transpose(0, 3, 1, 2, 4).reshape(
        DEPTH, N_LRU_BLOCKS, 1, 4 * LRU_BLOCK)
    ap_p = lru_a_param.reshape(DEPTH, 2, N_LRU_BLOCKS, LRU_BLOCK).transpose(0, 2, 1, 3).reshape(
        DEPTH, N_LRU_BLOCKS, 1, 2 * LRU_BLOCK)
    conv_b_p = conv_b.reshape(DEPTH, 1, LRU_W)
    wol, wom, wout = w_o_lru.astype(BF16), w_o_mla.astype(BF16), w_out.astype(BF16)
    fg, fu, fd = ffn_w_gate.astype(BF16), ffn_w_up.astype(BF16), ffn_w_down.astype(BF16)
    mg, mu, md = moe_w_gate.astype(BF16), moe_w_up.astype(BF16), moe_w_down
    router_hi = moe_router.astype(BF16)
    router_lo = (moe_router - router_hi.astype(F32)).astype(BF16)
    router = jnp.pad(jnp.concatenate([router_hi, router_lo], axis=-1),
                     ((0, 0), (0, 0), (0, LANES - 2 * N_EXPERTS)))
    n1, n2, gq, gkv = vec(norm1_g), vec(norm2_g), vec(q_norm_g), vec(kv_norm_g)

    xt = x.reshape(TOKENS, D_MODEL)
    for l in range(DEPTH):
        shift1, scale1, gate1, shift2, scale2, gate2 = [mod[l, :, k] for k in range(6)]
        q, k, v, u, y_gate, gb = _in_projection(l, xt, shift1, scale1, n1, w_in_p, cos_t, sin_t, gq, gkv,
                                                wq_p, wkv_p, hq, hk)
        attn = _attention(q, k, v).reshape(TOKENS, N_HEADS * V_DIM)
        h_lru = _lru_branch(l, u.reshape(BATCH, SEQ, LRU_W), conv_w, conv_b_p, wg_p, bg_p, ap_p)
        h_lru = h_lru.reshape(TOKENS, LRU_W)
        m = l // 2
        if l % 2 == 0:
            xt, h2 = _mix(l, h_lru, y_gate, attn, gb, xt, gate1, shift2, scale2, n2, wol, wom, wout)
            xt = _dense_ffn(m, h2, xt, gate2, fg, fu, fd)
        else:
            xt, h2, meta, wts, cnt = _mix(l, h_lru, y_gate, attn, gb, xt, gate1, shift2, scale2, n2, wol, wom,
                                          wout, router=router)
            pos0, pos1, tile_expert, tile_valid, (pad_start, pad_len) = _moe_plan(meta, cnt)
            xs = _dispatch(pos0, pos1, pad_start, pad_len, h2)
            y = _moe_ffn(m, tile_expert, tile_valid, xs, mg, mu, md)
            xt = _moe_combine(pos0, pos1, xt, gate2, wts, y)
    return xt.reshape(BATCH, SEQ, D_MODEL)
```

```python
import functools

import jax
import jax.numpy as jnp
from jax import lax
from jax.experimental import pallas as pl
from jax.experimental.pallas import tpu as pltpu

F32 = jnp.float32
BF16 = jnp.bfloat16

D_MODEL = 1024
BATCH = 8
SEQ = 2048
TOKENS = BATCH * SEQ
DEPTH = 4
N_HEADS = 8
NOPE = 64
ROPE = 32
QK_DIM = NOPE + ROPE
V_DIM = 64
Q_RANK = 256
KV_RANK = 256
LRU_W = D_MODEL
N_LRU_BLOCKS = 8
LRU_BLOCK = LRU_W // N_LRU_BLOCKS
RG_LRU_C = 8.0
D_FF = 2816
N_EXPERTS = 8
N_MOE_LAYERS = DEPTH // 2
EPS = 1e-6

LANES = 128
SUBLANES = 8
HEAD_TILE = LANES

MLA_COLS = Q_RANK + KV_RANK + 2 * HEAD_TILE
IN_COLS = MLA_COLS + 2 * LRU_W + 2 * D_MODEL

TM_PROJ = 512
TQ = 512
TM_FFN = 512
TF = D_FF // 2
N_MOE_TILES = (2 * TOKENS) // TM_FFN + N_EXPERTS
MOE_ROWS = N_MOE_TILES * TM_FFN
TM_DISPATCH = 512
TM_COMBINE = 256

MIB = 1024 * 1024


def _params(sem, vmem_mib):
    return pltpu.CompilerParams(dimension_semantics=sem, vmem_limit_bytes=vmem_mib * MIB)


def _dot(a, b):
    return jnp.dot(a, b, preferred_element_type=F32)


def _split_bf16(a):
    hi = a.astype(BF16)
    lo = (a - hi.astype(F32)).astype(BF16)
    return hi, lo


def _sigmoid(x):
    return 0.5 + 0.5 * jnp.tanh(0.5 * x)


def _silu(x):
    half = 0.5 * x
    return half + half * jnp.tanh(half)


def _rms(v, inv_n):
    return v * lax.rsqrt(jnp.sum(v * v, axis=-1, keepdims=True) * inv_n + EPS)


def _layer_spec(a, l):
    nd = a.ndim - 1
    return pl.BlockSpec((1,) + a.shape[1:], lambda i: (l,) + (0,) * nd)


TN_MOD = 1536


def _mod_kernel(c_ref, w_ref, b_ref, o_ref):
    c = c_ref[...]
    ch, cl = _split_bf16(c * jax.nn.sigmoid(c))
    wh, wl = _split_bf16(w_ref[0])
    o_ref[0] = _dot(ch, wh) + (_dot(ch, wl) + _dot(cl, wh)) + b_ref[0]


def _modulation(c, ada_w, ada_b):
    n = 6 * D_MODEL
    return pl.pallas_call(
        _mod_kernel,
        grid=(DEPTH, n // TN_MOD),
        in_specs=[
            pl.BlockSpec((BATCH, D_MODEL), lambda l, j: (0, 0)),
            pl.BlockSpec((1, D_MODEL, TN_MOD), lambda l, j: (l, 0, j)),
            pl.BlockSpec((1, 1, TN_MOD), lambda l, j: (l, 0, j)),
        ],
        out_specs=pl.BlockSpec((1, BATCH, TN_MOD), lambda l, j: (l, 0, j)),
        out_shape=jax.ShapeDtypeStruct((DEPTH, BATCH, n), F32),
        compiler_params=_params(("arbitrary", "arbitrary"), 40),
        name="adaln_mod",
    )(c, ada_w, ada_b.reshape(DEPTH, 1, n))


def _batch_vec_spec(tm):
    per_batch = SEQ // tm
    return pl.BlockSpec((1, 1, D_MODEL), lambda i: (i // per_batch, 0, 0))


def _resident_spec(a, l):
    nd = a.ndim - 1
    return pl.BlockSpec((1,) + a.shape[1:], lambda i: (l,) + (0,) * nd, pipeline_mode=pl.Buffered(1))


def _inproj_kernel(x_ref, sh_ref, sc_ref, g_ref, w_ref, cos_ref, sin_ref, gq_ref, gkv_ref, wq_ref, wkv_ref,
                   hq_ref, hk_ref, q_ref, k_ref, v_ref, u_ref, y_ref, gb_ref):
    x = x_ref[...]
    h = _rms(x, 1.0 / D_MODEL) * g_ref[0]
    h = (h * (1.0 + sc_ref[0]) + sh_ref[0]).astype(BF16)
    c0, c1, c2 = MLA_COLS, MLA_COLS + LRU_W, MLA_COLS + 2 * LRU_W
    m = _dot(h, w_ref[0, :, 0:c0])
    u_ref[...] = _dot(h, w_ref[0, :, c0:c1])
    y_ref[...] = _dot(h, w_ref[0, :, c1:c2]).astype(BF16)
    gb_ref[...] = _dot(h, w_ref[0, :, c2:IN_COLS]).astype(BF16)

    cos = cos_ref[...]
    sin = sin_ref[...]
    cq = m[:, 0:Q_RANK]
    ckv = m[:, Q_RANK:Q_RANK + KV_RANK]
    kr_a = m[:, Q_RANK + KV_RANK:Q_RANK + KV_RANK + HEAD_TILE]
    kr_b = m[:, Q_RANK + KV_RANK + HEAD_TILE:MLA_COLS]
    cqn = (_rms(cq, 1.0 / Q_RANK) * gq_ref[0]).astype(BF16)
    ckvn = (_rms(ckv, 1.0 / KV_RANK) * gkv_ref[0]).astype(BF16)
    qq = _dot(cqn, wq_ref[0])
    kv = _dot(ckvn, wkv_ref[0])
    k_pe = kr_a * cos + kr_b * sin
    nq = N_HEADS * HEAD_TILE
    lane = lax.broadcasted_iota(jnp.int32, (TM_PROJ, HEAD_TILE), 1)
    for hd in range(N_HEADS):
        lo, hi = hd * HEAD_TILE, (hd + 1) * HEAD_TILE
        qh = qq[:, lo:hi] * cos + qq[:, nq + lo:nq + hi] * sin
        q_ref[0, hd] = (_rms(qh, 1.0 / QK_DIM) * hq_ref[0]).astype(BF16)
        kh = kv[:, lo:hi] + k_pe
        k_ref[0, hd] = (_rms(kh, 1.0 / QK_DIM) * hk_ref[0]).astype(BF16)
        v_ref[0, hd] = jnp.where(lane == V_DIM * (1 - hd % 2), 1.0, kv[:, nq + lo:nq + hi]).astype(BF16)


def _in_projection(l, x, shift, scale, gain, w, cos_t, sin_t, gq, gkv, wq, wkv, hq, hk):
    tm = TM_PROJ
    per_batch = SEQ // tm
    row = lambda n: pl.BlockSpec((tm, n), lambda i: (i, 0))
    head = pl.BlockSpec((1, N_HEADS, tm, HEAD_TILE), lambda i: (i // per_batch, 0, i % per_batch, 0))
    head_shape = jax.ShapeDtypeStruct((BATCH, N_HEADS, SEQ, HEAD_TILE), BF16)
    params = [gq, gkv, wq, wkv, hq, hk]
    return pl.pallas_call(
        _inproj_kernel,
        grid=(TOKENS // tm,),
        in_specs=[row(D_MODEL), _batch_vec_spec(tm), _batch_vec_spec(tm), _layer_spec(gain, l),
                  _resident_spec(w, l), row(HEAD_TILE), row(HEAD_TILE)] + [_resident_spec(p, l) for p in params],
        out_specs=[head, head, head, row(LRU_W), row(LRU_W), row(2 * D_MODEL)],
        out_shape=[
            head_shape, head_shape, head_shape,
            jax.ShapeDtypeStruct((TOKENS, LRU_W), F32),
            jax.ShapeDtypeStruct((TOKENS, LRU_W), BF16),
            jax.ShapeDtypeStruct((TOKENS, 2 * D_MODEL), BF16),
        ],
        compiler_params=_params(("arbitrary",), 56),
        name="in_proj",
    )(x, shift, scale, gain, w, cos_t, sin_t, *params)


def _attn_kernel(q_ref, k_ref, v_ref, o_ref):
    lane = lax.broadcasted_iota(jnp.int32, (TQ, LANES), 1)
    for pair in range(N_HEADS // 2):
        acc = None
        for j in range(2):
            h = 2 * pair + j
            s = lax.dot_general(q_ref[0, h], k_ref[0, h], (((1,), (1,)), ((), ())),
                                preferred_element_type=F32)
            p = jnp.exp2(s - jnp.max(s, axis=-1, keepdims=True)).astype(BF16)
            o = _dot(p, v_ref[0, h])
            ones_lane = V_DIM * (1 - j)
            inv = 1.0 / o[:, ones_lane:ones_lane + 1]
            mine = (lane < V_DIM) if j == 0 else (lane >= V_DIM)
            o = jnp.where(mine, o * inv, 0.0)
            acc = o if acc is None else acc + o
        o_ref[0, :, pair * LANES:(pair + 1) * LANES] = acc.astype(BF16)


def _attention(q, k, v):
    nq = SEQ // TQ
    kv_spec = pl.BlockSpec((1, N_HEADS, SEQ, HEAD_TILE), lambda b, i: (b, 0, 0, 0))
    return pl.pallas_call(
        _attn_kernel,
        grid=(BATCH, nq),
        in_specs=[pl.BlockSpec((1, N_HEADS, TQ, HEAD_TILE), lambda b, i: (b, 0, i, 0)), kv_spec, kv_spec],
        out_specs=pl.BlockSpec((1, TQ, N_HEADS * V_DIM), lambda b, i: (b, i, 0)),
        out_shape=jax.ShapeDtypeStruct((BATCH, SEQ, N_HEADS * V_DIM), BF16),
        compiler_params=_params(("arbitrary", "arbitrary"), 48),
        name="attention",
    )(q, k, v)


CHUNK = SEQ // SUBLANES
PITCH = CHUNK + 4
CONV_BEFORE, CONV_AFTER = 2, 1
TWO_LOG2E = 2.0 * 1.4426950408889634


def _shift_rows(v, row, down):
    if down:
        return jnp.where(row == 0, 0.0, pltpu.roll(v, 1, 0))
    return jnp.where(row == SUBLANES - 1, 0.0, pltpu.roll(v, SUBLANES - 1, 0))


def _scan_sublanes(a, b, row, reverse):
    for s in (1, 2, 4):
        shift = SUBLANES - s if reverse else s
        a_s = pltpu.roll(a, shift, 0)
        b_s = pltpu.roll(b, shift, 0)
        keep = (row < SUBLANES - s) if reverse else (row >= s)
        b = jnp.where(keep, a * b_s + b, b)
        a = jnp.where(keep, a * a_s, a)
    return b


def _lru_kernel(u_hbm, cw_ref, cb_ref, wg_ref, bg_ref, ap_ref, h_hbm,
                uin_ref, hout_ref, up_ref, uc_ref, g_ref, a_ref, b_ref, h_ref, p_ref, in_sem, out_sem):
    n_steps = pl.num_programs(0) * N_LRU_BLOCKS
    step = pl.program_id(0) * N_LRU_BLOCKS + pl.program_id(1)
    slot = step % 2

    def chunk_copies(s, sl, inbound):
        bb = s // N_LRU_BLOCKS
        col = pl.multiple_of((s % N_LRU_BLOCKS) * LRU_BLOCK, LRU_BLOCK)
        out = []
        for j in range(SUBLANES):
            hbm = (u_hbm if inbound else h_hbm).at[bb, pl.ds(j * CHUNK, CHUNK), pl.ds(col, LRU_BLOCK)]
            if inbound:
                out.append(pltpu.make_async_copy(hbm, uin_ref.at[sl, pl.ds(j * PITCH, CHUNK), :], in_sem.at[sl]))
            else:
                out.append(pltpu.make_async_copy(hout_ref.at[sl, pl.ds(j * PITCH, CHUNK), :], hbm, out_sem.at[sl]))
        return out

    @pl.when(step == 0)
    def _():
        for cp in chunk_copies(step, slot, True):
            cp.start()

    @pl.when(step + 1 < n_steps)
    def _():
        for cp in chunk_copies(step + 1, 1 - slot, True):
            cp.start()

    for cp in chunk_copies(step, slot, True):
        cp.wait()

    row = lax.broadcasted_iota(jnp.int32, (SUBLANES, LRU_BLOCK), 0)

    def tile_rows(k):
        return slice(k * SUBLANES, (k + 1) * SUBLANES)

    for k in range(CHUNK):
        up_ref[tile_rows(k + CONV_BEFORE), :] = uin_ref[slot, pl.ds(k, SUBLANES, stride=PITCH), :]
    for i in range(CONV_BEFORE):
        src = up_ref[(CHUNK + i) * SUBLANES:(CHUNK + i + 1) * SUBLANES, :]
        up_ref[i * SUBLANES:(i + 1) * SUBLANES, :] = _shift_rows(src, row, True)
    for i in range(CONV_AFTER):
        src = up_ref[(CONV_BEFORE + i) * SUBLANES:(CONV_BEFORE + i + 1) * SUBLANES, :]
        dst = (CONV_BEFORE + CHUNK + i) * SUBLANES
        up_ref[dst:dst + SUBLANES, :] = _shift_rows(src, row, False)

    uc = cb_ref[0]
    for tap in range(CONV_BEFORE + CONV_AFTER + 1):
        uc = uc + up_ref[tap * SUBLANES:tap * SUBLANES + SEQ, :] * cw_ref[0, tap:tap + 1, :]
    uc_ref[...] = uc
    g_ref[...] = _dot(uc.astype(BF16), wg_ref[0, 0]) + bg_ref[0, 0]

    sp = jax.nn.softplus(ap_ref[0, 0])
    u_half = 0.5 * uc_ref[...]
    for d in range(2):
        quarter_coef = (-0.25 * RG_LRU_C) * sp[:, d * LRU_BLOCK:(d + 1) * LRU_BLOCK]
        g_r = g_ref[:, 2 * d * LRU_BLOCK:(2 * d + 1) * LRU_BLOCK]
        g_i = g_ref[:, (2 * d + 1) * LRU_BLOCK:(2 * d + 2) * LRU_BLOCK]
        x = quarter_coef + quarter_coef * jnp.tanh(g_r)
        a = jnp.exp2(TWO_LOG2E * x)
        s = -jnp.tanh(x)
        mult = (1.0 + a) * jnp.where(s > 0.0, s * lax.rsqrt(s), 0.0)
        gated = u_half + u_half * jnp.tanh(g_i)
        a_ref[d] = a
        b_ref[d] = mult * gated
        first = slice(0, SUBLANES) if d == 0 else slice(SEQ - SUBLANES, SEQ)
        first_row = 0 if d == 0 else SUBLANES - 1
        g_first = u_half[first, :] + u_half[first, :] * jnp.tanh(g_i[first, :])
        b_ref[d, first, :] = jnp.where(row == first_row, g_first, b_ref[d, first, :])

    zeros = jnp.zeros((SUBLANES, LRU_BLOCK), F32)
    ones = jnp.ones((SUBLANES, LRU_BLOCK), F32)
    state = [(zeros, ones), (zeros, ones)]
    for m in range(CHUNK // 2):
        for d in range(2):
            k0, k1 = (2 * m, 2 * m + 1) if d == 0 else (CHUNK - 1 - 2 * m, CHUNK - 2 - 2 * m)
            r0, r1 = tile_rows(k0), tile_rows(k1)
            a0, b0 = a_ref[d, r0, :], b_ref[d, r0, :]
            a1, b1 = a_ref[d, r1, :], b_ref[d, r1, :]
            a01 = a1 * a0
            b01 = a1 * b0 + b1
            h, p = state[d]
            h_ref[d, r0, :] = a0 * h + b0
            p_ref[d, r0, :] = a0 * p
            h = a01 * h + b01
            p = a01 * p
            h_ref[d, r1, :] = h
            p_ref[d, r1, :] = p
            state[d] = (h, p)
    (hf, af), (hb, ab) = state
    carry_f = _shift_rows(_scan_sublanes(af, hf, row, False), row, True)
    carry_b = _shift_rows(_scan_sublanes(ab, hb, row, True), row, False)

    @pl.when(step >= 2)
    def _():
        for cp in chunk_copies(step, slot, False):
            cp.wait()

    for k in range(CHUNK):
        r = tile_rows(k)
        h = (h_ref[0, r, :] + p_ref[0, r, :] * carry_f) + (h_ref[1, r, :] + p_ref[1, r, :] * carry_b)
        hout_ref[slot, pl.ds(k, SUBLANES, stride=PITCH), :] = h
    for cp in chunk_copies(step, slot, False):
        cp.start()

    @pl.when(step == n_steps - 1)
    def _():
        for cp in chunk_copies(step, 1 - slot, False) + chunk_copies(step, slot, False):
            cp.wait()


def _lru_branch(l, u, conv_w, conv_b, wg, bg, a_param):
    batch = u.shape[0]
    vm = lambda *shape: pltpu.VMEM(shape, F32)
    stage = (2, SUBLANES * PITCH, LRU_BLOCK)
    return pl.pallas_call(
        _lru_kernel,
        grid=(batch, N_LRU_BLOCKS),
        in_specs=[
            pl.BlockSpec(memory_space=pl.ANY),
            pl.BlockSpec((1, 4, LRU_BLOCK), lambda b, n: (l, 0, n)),
            pl.BlockSpec((1, 1, LRU_BLOCK), lambda b, n: (l, 0, n)),
            pl.BlockSpec((1, 1, LRU_BLOCK, 4 * LRU_BLOCK), lambda b, n: (l, n, 0, 0)),
            pl.BlockSpec((1, 1, 1, 4 * LRU_BLOCK), lambda b, n: (l, n, 0, 0)),
            pl.BlockSpec((1, 1, 1, 2 * LRU_BLOCK), lambda b, n: (l, n, 0, 0)),
        ],
        out_specs=pl.BlockSpec(memory_space=pl.ANY),
        out_shape=jax.ShapeDtypeStruct((batch, SEQ, LRU_W), F32),
        scratch_shapes=[vm(*stage), vm(*stage),
                        vm(SEQ + (CONV_BEFORE + CONV_AFTER) * SUBLANES, LRU_BLOCK), vm(SEQ, LRU_BLOCK),
                        vm(SEQ, 4 * LRU_BLOCK)] + [vm(2, SEQ, LRU_BLOCK)] * 4 + [
                        pltpu.SemaphoreType.DMA((2,)), pltpu.SemaphoreType.DMA((2,))],
        compiler_params=_params(("arbitrary", "arbitrary"), 40),
        name="rg_lru",
    )(u, conv_w, conv_b, wg, bg, a_param)


def _gelu_tanh(x):
    return 0.5 * x * (1.0 + jnp.tanh(0.7978845608028654 * (x + 0.044715 * x * x * x)))


def _route(lg, meta_ref, wt_ref, cnt_ref, carry_ref):
    tm = lg.shape[0]

    @pl.when(pl.program_id(0) == 0)
    def _():
        carry_ref[...] = jnp.zeros_like(carry_ref)

    lane = lax.broadcasted_iota(jnp.int32, (tm, LANES), 1)
    lane_f = lane.astype(F32)
    neg = jnp.float32(-jnp.inf)
    lg = jnp.where(lane < N_EXPERTS, lg, neg)
    m1 = jnp.max(lg, axis=-1, keepdims=True)
    i1 = jnp.min(jnp.where(lg == m1, lane_f, float(LANES)), axis=-1, keepdims=True).astype(jnp.int32)
    lg2 = jnp.where(lane == i1, neg, lg)
    m2 = jnp.max(lg2, axis=-1, keepdims=True)
    i2 = jnp.min(jnp.where(lg2 == m2, lane_f, float(LANES)), axis=-1, keepdims=True).astype(jnp.int32)
    e = jnp.exp(m2 - m1)
    w1 = 1.0 / (1.0 + e)
    w2 = e * w1
    sel = jnp.where(lane == i1, 1.0, jnp.where(lane == i2, 1.0, 0.0))
    r_i = lax.broadcasted_iota(jnp.int32, (tm, tm), 0)
    c_i = lax.broadcasted_iota(jnp.int32, (tm, tm), 1)
    tri = jnp.where(r_i > c_i, 1.0, 0.0).astype(BF16)
    cum = _dot(tri, sel.astype(BF16)) + carry_ref[0:1, :]
    r1 = jnp.sum(jnp.where(lane == i1, cum, 0.0), axis=-1, keepdims=True).astype(jnp.int32)
    r2 = jnp.sum(jnp.where(lane == i2, cum, 0.0), axis=-1, keepdims=True).astype(jnp.int32)
    total = carry_ref[0:1, :] + jnp.sum(sel, axis=0, keepdims=True)
    carry_ref[...] = jnp.broadcast_to(total, carry_ref.shape)
    cnt_ref[...] = jnp.broadcast_to(total, cnt_ref.shape)
    meta_ref[...] = jnp.where(lane == 0, i1, jnp.where(lane == 1, i2, jnp.where(lane == 2, r1,
                              jnp.where(lane == 3, r2, 0))))
    wt_ref[...] = jnp.where(lane == 0, w1, jnp.where(lane == 1, w2, 0.0))


def _mix_kernel(*refs, moe):
    if moe:
        (h_ref, y_ref, at_ref, gb_ref, x_ref, g1_ref, sh_ref, sc_ref, n2_ref, wol_ref, wom_ref, wout_ref,
         wr_ref, xo_ref, h2_ref, meta_ref, wt_ref, cnt_ref, carry_ref) = refs
    else:
        (h_ref, y_ref, at_ref, gb_ref, x_ref, g1_ref, sh_ref, sc_ref, n2_ref, wol_ref, wom_ref, wout_ref,
         xo_ref, h2_ref) = refs
    hg = (h_ref[...] * _gelu_tanh(y_ref[...].astype(F32))).astype(BF16)
    y_lru = _dot(hg, wol_ref[0])
    y_mla = _dot(at_ref[...], wom_ref[0])
    g_lru = _sigmoid(gb_ref[:, 0:D_MODEL].astype(F32))
    g_mla = _sigmoid(gb_ref[:, D_MODEL:2 * D_MODEL].astype(F32))
    z = (g_lru * y_lru + g_mla * y_mla).astype(BF16)
    xn = x_ref[...] + g1_ref[0] * _dot(z, wout_ref[0])
    xo_ref[...] = xn
    h2 = _rms(xn, 1.0 / D_MODEL) * n2_ref[0]
    h2 = h2 * (1.0 + sc_ref[0]) + sh_ref[0]
    if moe:
        h2_ref[...] = h2
        hh, hl = _split_bf16(h2)
        parts = _dot(hh, wr_ref[0]) + _dot(hl, wr_ref[0])
        _route(parts + pltpu.roll(parts, LANES - N_EXPERTS, 1), meta_ref, wt_ref, cnt_ref, carry_ref)
    else:
        h2_ref[...] = h2.astype(BF16)


def _mix(l, h_lru, y_gate, attn, gb, x, gate1, shift2, scale2, n2, wol, wom, wout, router=None):
    tm = TM_PROJ
    moe = router is not None
    row = lambda n: pl.BlockSpec((tm, n), lambda i: (i, 0))
    vec = _batch_vec_spec(tm)
    ins = [h_lru, y_gate, attn, gb, x, gate1, shift2, scale2, n2, wol, wom, wout]
    in_specs = [row(LRU_W), row(LRU_W), row(N_HEADS * V_DIM), row(2 * D_MODEL), row(D_MODEL), vec, vec, vec,
                _layer_spec(n2, l), _layer_spec(wol, l), _layer_spec(wom, l), _layer_spec(wout, l)]
    out_specs = [row(D_MODEL), row(D_MODEL)]
    out_shape = [jax.ShapeDtypeStruct((TOKENS, D_MODEL), F32),
                 jax.ShapeDtypeStruct((TOKENS, D_MODEL), F32 if moe else BF16)]
    scratch = []
    if moe:
        ins.append(router)
        in_specs.append(_layer_spec(router, l // 2))
        out_specs += [row(LANES), row(LANES), pl.BlockSpec((SUBLANES, LANES), lambda i: (0, 0))]
        out_shape += [jax.ShapeDtypeStruct((TOKENS, LANES), jnp.int32),
                      jax.ShapeDtypeStruct((TOKENS, LANES), F32),
                      jax.ShapeDtypeStruct((SUBLANES, LANES), F32)]
        scratch.append(pltpu.VMEM((SUBLANES, LANES), F32))
    return pl.pallas_call(
        functools.partial(_mix_kernel, moe=moe),
        grid=(TOKENS // tm,),
        in_specs=in_specs,
        out_specs=out_specs,
        out_shape=out_shape,
        scratch_shapes=scratch,
        compiler_params=_params(("arbitrary",), 48),
        name="mix_moe" if moe else "mix_dense",
    )(*ins)


def _swiglu_accumulate(h, wg, wu, wd, acc_ref, j, finish):
    a = _dot(h, wg)
    b = _dot(h, wu)
    part = _dot((_silu(a) * b).astype(BF16), wd)
    last = pl.num_programs(1) - 1

    @pl.when(j == 0)
    def _():
        acc_ref[...] = part

    @pl.when(jnp.logical_and(j > 0, j < last))
    def _():
        acc_ref[...] += part

    @pl.when(j == last)
    def _():
        finish(acc_ref[...] + part)


def _ffn_kernel(h_ref, x_ref, g2_ref, wg_ref, wu_ref, wd_ref, o_ref, acc_ref):
    def finish(total):
        o_ref[...] = x_ref[...] + g2_ref[0] * total

    _swiglu_accumulate(h_ref[...], wg_ref[0], wu_ref[0], wd_ref[0], acc_ref, pl.program_id(1), finish)


def _dense_ffn(m, h2, x, gate2, wg, wu, wd):
    tm = TM_FFN
    per_batch = SEQ // tm
    row = pl.BlockSpec((tm, D_MODEL), lambda i, j: (i, 0))
    return pl.pallas_call(
        _ffn_kernel,
        grid=(TOKENS // tm, D_FF // TF),
        in_specs=[row, row, pl.BlockSpec((1, 1, D_MODEL), lambda i, j: (i // per_batch, 0, 0)),
                  pl.BlockSpec((1, D_MODEL, TF), lambda i, j: (m, 0, j)),
                  pl.BlockSpec((1, D_MODEL, TF), lambda i, j: (m, 0, j)),
                  pl.BlockSpec((1, TF, D_MODEL), lambda i, j: (m, j, 0))],
        out_specs=row,
        out_shape=jax.ShapeDtypeStruct((TOKENS, D_MODEL), F32),
        scratch_shapes=[pltpu.VMEM((tm, D_MODEL), F32)],
        compiler_params=_params(("arbitrary", "arbitrary"), 56),
        name="ffn_dense",
    )(h2, x, gate2, wg, wu, wd)


def _row_copy(src_ref, src_row, dst_ref, dst_row, sem):
    return pltpu.make_async_copy(src_ref.at[pl.ds(src_row, 1)], dst_ref.at[pl.ds(dst_row, 1)], sem)


def _dispatch_kernel(p0_ref, p1_ref, pad_start_ref, pad_len_ref, h_hbm, xs_ref, buf_ref, zero_ref,
                     load_sem, row_sem, pad_sem):
    tm = TM_DISPATCH
    step = pl.program_id(0)
    n_steps = pl.num_programs(0)
    slot = step % 2
    base = step * tm

    def load(tile, sl):
        return pltpu.make_async_copy(h_hbm.at[pl.ds(tile * tm, tm)], buf_ref.at[sl], load_sem.at[sl])

    def drain_rows():
        for _ in range(2):
            pltpu.make_async_copy(buf_ref.at[0], xs_ref.at[pl.ds(0, tm)], row_sem).wait()

    def pad_copies(act):
        for e in range(N_EXPERTS):
            start = pad_start_ref[e]
            length = pad_len_ref[e]
            head = jnp.minimum((0 - start) & (SUBLANES - 1), length)

            def one_row(i, c):
                act(_row_copy(zero_ref, 0, xs_ref, start + i, pad_sem))
                return c

            def one_block(i, c):
                first = pl.multiple_of(start + head + i * SUBLANES, SUBLANES)
                act(pltpu.make_async_copy(zero_ref, xs_ref.at[pl.ds(first, SUBLANES)], pad_sem))
                return c

            lax.fori_loop(0, head, one_row, 0)
            lax.fori_loop(0, (length - head) // SUBLANES, one_block, 0)

    @pl.when(step == 0)
    def _():
        load(step, slot).start()
        zero_ref[...] = jnp.zeros_like(zero_ref)
        pad_copies(lambda cp: cp.start())

    @pl.when(step > 0)
    def _():
        drain_rows()

    @pl.when(step + 1 < n_steps)
    def _():
        load(step + 1, 1 - slot).start()

    load(step, slot).wait()
    src = buf_ref.at[slot]

    for r in range(tm):
        _row_copy(src, r, xs_ref, p0_ref[base + r], row_sem).start(priority=r % 2)
        _row_copy(src, r, xs_ref, p1_ref[base + r], row_sem).start(priority=(r + 1) % 2)

    @pl.when(step == n_steps - 1)
    def _():
        drain_rows()
        pad_copies(lambda cp: cp.wait())


def _dispatch(pos0, pos1, pad_start, pad_len, h2):
    tm = TM_DISPATCH
    grid_spec = pltpu.PrefetchScalarGridSpec(
        num_scalar_prefetch=4,
        grid=(TOKENS // tm,),
        in_specs=[pl.BlockSpec(memory_space=pl.ANY)],
        out_specs=pl.BlockSpec(memory_space=pl.ANY),
        scratch_shapes=[pltpu.VMEM((2, tm, D_MODEL), F32), pltpu.VMEM((SUBLANES, D_MODEL), F32),
                        pltpu.SemaphoreType.DMA((2,)), pltpu.SemaphoreType.DMA(()), pltpu.SemaphoreType.DMA(())],
    )
    return pl.pallas_call(
        _dispatch_kernel,
        grid_spec=grid_spec,
        out_shape=jax.ShapeDtypeStruct((MOE_ROWS, D_MODEL), F32),
        compiler_params=_params(("arbitrary",), 32),
        name="moe_dispatch",
    )(pos0, pos1, pad_start, pad_len, h2)


def _moe_ffn_kernel(te_ref, valid_ref, x_ref, wg_ref, wu_ref, wd_ref, o_ref, acc_ref):
    i = pl.program_id(0)
    j = pl.program_id(1)
    last = j == pl.num_programs(1) - 1

    @pl.when(valid_ref[i] == 1)
    def _():
        def finish(total):
            o_ref[...] = total

        _swiglu_accumulate(x_ref[...].astype(BF16), wg_ref[0, 0], wu_ref[0, 0], wd_ref[0, 0].astype(BF16),
                           acc_ref, j, finish)

    @pl.when(jnp.logical_and(valid_ref[i] == 0, last))
    def _():
        o_ref[...] = jnp.zeros_like(o_ref)


def _moe_ffn(m, tile_expert, tile_valid, xs, wg, wu, wd):
    tm = TM_FFN
    row = pl.BlockSpec((tm, D_MODEL), lambda i, j, te, va: (i, 0))
    grid_spec = pltpu.PrefetchScalarGridSpec(
        num_scalar_prefetch=2,
        grid=(N_MOE_TILES, D_FF // TF),
        in_specs=[row,
                  pl.BlockSpec((1, 1, D_MODEL, TF), lambda i, j, te, va: (m, te[i], 0, j * va[i])),
                  pl.BlockSpec((1, 1, D_MODEL, TF), lambda i, j, te, va: (m, te[i], 0, j * va[i])),
                  pl.BlockSpec((1, 1, TF, D_MODEL), lambda i, j, te, va: (m, te[i], j * va[i], 0))],
        out_specs=row,
        scratch_shapes=[pltpu.VMEM((tm, D_MODEL), F32)],
    )
    return pl.pallas_call(
        _moe_ffn_kernel,
        grid_spec=grid_spec,
        out_shape=jax.ShapeDtypeStruct((MOE_ROWS, D_MODEL), F32),
        compiler_params=_params(("arbitrary", "arbitrary"), 56),
        name="ffn_moe",
    )(tile_expert, tile_valid, xs, wg, wu, wd)


def _moe_combine_kernel(p0_ref, p1_ref, x_ref, g2_ref, wt_ref, y_ref, o_ref, buf_ref, sem):
    tm = TM_COMBINE
    step = pl.program_id(0)
    slot = step % 2

    def gather(tile, sl):
        base = tile * tm

        for r in range(tm):
            _row_copy(y_ref, p0_ref[base + r], buf_ref.at[sl, 0], r, sem.at[sl]).start(priority=r % 2)
            _row_copy(y_ref, p1_ref[base + r], buf_ref.at[sl, 1], r, sem.at[sl]).start(priority=(r + 1) % 2)

    @pl.when(step == 0)
    def _():
        gather(step, slot)

    @pl.when(step + 1 < pl.num_programs(0))
    def _():
        gather(step + 1, 1 - slot)

    for k in range(2):
        pltpu.make_async_copy(y_ref.at[pl.ds(0, tm)], buf_ref.at[slot, k], sem.at[slot]).wait()
    w = wt_ref[...]
    f = w[:, 0:1] * buf_ref[slot, 0] + w[:, 1:2] * buf_ref[slot, 1]
    o_ref[...] = x_ref[...] + g2_ref[0] * f


def _moe_combine(pos0, pos1, x, gate2, wts, y):
    tm = TM_COMBINE
    per_batch = SEQ // tm
    row = pl.BlockSpec((tm, D_MODEL), lambda i, p0, p1: (i, 0))
    grid_spec = pltpu.PrefetchScalarGridSpec(
        num_scalar_prefetch=2,
        grid=(TOKENS // tm,),
        in_specs=[row,
                  pl.BlockSpec((1, 1, D_MODEL), lambda i, p0, p1: (i // per_batch, 0, 0)),
                  pl.BlockSpec((tm, LANES), lambda i, p0, p1: (i, 0)),
                  pl.BlockSpec(memory_space=pl.ANY)],
        out_specs=row,
        scratch_shapes=[pltpu.VMEM((2, 2, tm, D_MODEL), F32), pltpu.SemaphoreType.DMA((2,))],
    )
    return pl.pallas_call(
        _moe_combine_kernel,
        grid_spec=grid_spec,
        out_shape=jax.ShapeDtypeStruct((TOKENS, D_MODEL), F32),
        compiler_params=_params(("arbitrary",), 32),
        name="moe_combine",
    )(pos0, pos1, x, gate2, wts, y)


def _prep_w_in(w_in):
    cq_ckv = w_in[..., 0:Q_RANK + KV_RANK]
    kr = w_in[..., Q_RANK + KV_RANK:Q_RANK + KV_RANK + ROPE]
    rest = w_in[..., Q_RANK + KV_RANK + ROPE:]
    x1, x2 = kr[..., :ROPE // 2], kr[..., ROPE // 2:]
    z = lambda n: jnp.zeros(kr.shape[:-1] + (n,), w_in.dtype)
    kr_a = jnp.concatenate([z(NOPE), x1, x2, z(HEAD_TILE - QK_DIM)], axis=-1)
    kr_b = jnp.concatenate([z(NOPE), -x2, x1, z(HEAD_TILE - QK_DIM)], axis=-1)
    return jnp.concatenate([cq_ckv, kr_a, kr_b, rest], axis=-1).astype(BF16)


def _prep_w_uq(w_uq):
    w = w_uq.reshape(DEPTH, Q_RANK, N_HEADS, QK_DIM)
    nope, x1, x2 = w[..., :NOPE], w[..., NOPE:NOPE + ROPE // 2], w[..., NOPE + ROPE // 2:]
    z = lambda n: jnp.zeros(w.shape[:-1] + (n,), w.dtype)
    qa = jnp.concatenate([nope, x1, x2, z(HEAD_TILE - QK_DIM)], axis=-1)
    qb = jnp.concatenate([z(NOPE), -x2, x1, z(HEAD_TILE - QK_DIM)], axis=-1)
    flat = lambda a: a.reshape(DEPTH, Q_RANK, N_HEADS * HEAD_TILE)
    return jnp.concatenate([flat(qa), flat(qb)], axis=-1).astype(BF16)


def _prep_w_ukv(w_ukv):
    w = w_ukv.reshape(DEPTH, KV_RANK, N_HEADS, NOPE + V_DIM)
    k_nope, v = w[..., :NOPE], w[..., NOPE:]
    zk = jnp.zeros(k_nope.shape[:-1] + (HEAD_TILE - NOPE,), w.dtype)
    ka = jnp.concatenate([k_nope, zk], axis=-1)
    zv = jnp.zeros_like(v)
    even = (jnp.arange(N_HEADS) % 2 == 0)[None, None, :, None]
    vp = jnp.concatenate([jnp.where(even, v, zv), jnp.where(even, zv, v)], axis=-1)
    flat = lambda a: a.reshape(DEPTH, KV_RANK, N_HEADS * HEAD_TILE)
    return jnp.concatenate([flat(ka), flat(vp)], axis=-1).astype(BF16)


def _head_gain(g, scale):
    pad = jnp.zeros((DEPTH, HEAD_TILE - QK_DIM), g.dtype)
    return (jnp.concatenate([g, pad], axis=-1) * scale).reshape(DEPTH, 1, HEAD_TILE)


def _rope_tables(positions):
    inv_freq = 1.0 / (10000.0 ** (jnp.arange(0, ROPE, 2, dtype=F32) / ROPE))
    ang = positions.astype(F32).reshape(TOKENS, 1) * inv_freq
    cos, sin = jnp.cos(ang), jnp.sin(ang)
    cos_t = jnp.concatenate([jnp.ones((TOKENS, NOPE), F32), cos, cos,
                             jnp.zeros((TOKENS, HEAD_TILE - QK_DIM), F32)], axis=-1)
    sin_t = jnp.concatenate([jnp.zeros((TOKENS, NOPE), F32), sin, sin,
                             jnp.zeros((TOKENS, HEAD_TILE - QK_DIM), F32)], axis=-1)
    return cos_t, sin_t


def _moe_plan(meta, cnt):
    e1, e2, r1, r2 = meta[:, 0], meta[:, 1], meta[:, 2], meta[:, 3]
    counts = cnt[0, :N_EXPERTS].astype(jnp.int32)
    padded = ((counts + TM_FFN - 1) // TM_FFN) * TM_FFN
    ends = jnp.cumsum(padded)
    starts = ends - padded
    experts = jnp.arange(N_EXPERTS, dtype=jnp.int32)
    start_of = lambda e: jnp.sum(jnp.where(e[:, None] == experts[None, :], starts[None, :], 0), axis=1)
    pos0 = start_of(e1) + r1
    pos1 = start_of(e2) + r2
    tile_start = jnp.arange(N_MOE_TILES, dtype=jnp.int32) * TM_FFN
    tile_expert = jnp.minimum(jnp.sum(tile_start[:, None] >= ends[None, :], axis=1), N_EXPERTS - 1)
    tile_valid = (tile_start < ends[-1]).astype(jnp.int32)
    group_end = jnp.where(experts == N_EXPERTS - 1, MOE_ROWS, ends)
    pads = (starts + counts, group_end - (starts + counts))
    return pos0.astype(jnp.int32), pos1.astype(jnp.int32), tile_expert.astype(jnp.int32), tile_valid, pads


def kernel(x, c, positions, ada_w, ada_b, norm1_g, norm2_g, w_in, q_norm_g, kv_norm_g, w_uq, w_ukv, q_head_g,
           k_head_g, w_o_mla, conv_w, conv_b, lru_gate_w, lru_gate_b, lru_a_param, w_o_lru, w_out, ffn_w_gate,
           ffn_w_up, ffn_w_down, moe_router, moe_w_gate, moe_w_up, moe_w_down):
    cos_t, sin_t = _rope_tables(positions)
    mod = _modulation(c, ada_w, ada_b).reshape(DEPTH, BATCH, 6, 1, D_MODEL)

    vec = lambda a: a.reshape(DEPTH, 1, a.shape[-1])
    w_in_p = _prep_w_in(w_in)
    wq_p = _prep_w_uq(w_uq)
    wkv_p = _prep_w_ukv(w_ukv)
    hq = _head_gain(q_head_g, 0.5 * TWO_LOG2E * QK_DIM ** -0.5)
    hk = _head_gain(k_head_g, 1.0)
    wg_p = (0.5 * lru_gate_w).transpose(0, 3, 4, 1, 2, 5).reshape(
        DEPTH, N_LRU_BLOCKS, LRU_BLOCK, 4 * LRU_BLOCK).astype(BF16)
    bg_p = (0.5 * lru_gate_b).reshape(DEPTH, 2, 2, N_LRU_BLOCKS, LRU_BLOCK).transpose(0, 3, 1, 2, 4).reshape(
        DEPTH, N_LRU_BLOCKS, 1, 4 * LRU_BLOCK)
    ap_p = lru_a_param.reshape(DEPTH, 2, N_LRU_BLOCKS, LRU_BLOCK).transpose(0, 2, 1, 3).reshape(
        DEPTH, N_LRU_BLOCKS, 1, 2 * LRU_BLOCK)
    conv_b_p = conv_b.reshape(DEPTH, 1, LRU_W)
    wol, wom, wout = w_o_lru.astype(BF16), w_o_mla.astype(BF16), w_out.astype(BF16)
    fg, fu, fd = ffn_w_gate.astype(BF16), ffn_w_up.astype(BF16), ffn_w_down.astype(BF16)
    mg, mu, md = moe_w_gate.astype(BF16), moe_w_up.astype(BF16), moe_w_down
    router_hi = moe_router.astype(BF16)
    router_lo = (moe_router - router_hi.astype(F32)).astype(BF16)
    router = jnp.pad(jnp.concatenate([router_hi, router_lo], axis=-1),
                     ((0, 0), (0, 0), (0, LANES - 2 * N_EXPERTS)))
    n1, n2, gq, gkv = vec(norm1_g), vec(norm2_g), vec(q_norm_g), vec(kv_norm_g)

    xt = x.reshape(TOKENS, D_MODEL)
    for l in range(DEPTH):
        shift1, scale1, gate1, shift2, scale2, gate2 = [mod[l, :, k] for k in range(6)]
        q, k, v, u, y_gate, gb = _in_projection(l, xt, shift1, scale1, n1, w_in_p, cos_t, sin_t, gq, gkv,
                                                wq_p, wkv_p, hq, hk)
        attn = _attention(q, k, v).reshape(TOKENS, N_HEADS * V_DIM)
        h_lru = _lru_branch(l, u.reshape(BATCH, SEQ, LRU_W), conv_w, conv_b_p, wg_p, bg_p, ap_p)
        h_lru = h_lru.reshape(TOKENS, LRU_W)
        m = l // 2
        if l % 2 == 0:
            xt, h2 = _mix(l, h_lru, y_gate, attn, gb, xt, gate1, shift2, scale2, n2, wol, wom, wout)
            xt = _dense_ffn(m, h2, xt, gate2, fg, fu, fd)
        else:
            xt, h2, meta, wts, cnt = _mix(l, h_lru, y_gate, attn, gb, xt, gate1, shift2, scale2, n2, wol, wom,
                                          wout, router=router)
            pos0, pos1, tile_expert, tile_valid, (pad_start, pad_len) = _moe_plan(meta, cnt)
            xs = _dispatch(pos0, pos1, pad_start, pad_len, h2)
            y = _moe_ffn(m, tile_expert, tile_valid, xs, mg, mu, md)
            xt = _moe_combine(pos0, pos1, xt, gate2, wts, y)
    return xt.reshape(BATCH, SEQ, D_MODEL)
```
